```python
import math
import jax
import jax.numpy as jnp
from jax import lax
import numpy as np

D_MODEL = 1024
BATCH = 8
SEQ = 4096
DEPTH = 1

HYENA_WIDTH = D_MODEL // 2
ATTN_WIDTH = D_MODEL - HYENA_WIDTH
ATTN_HEAD_DIM = 64
ATTN_HEADS = ATTN_WIDTH // (2 * ATTN_HEAD_DIM)
IN_WIDTH = 3 * HYENA_WIDTH + 3 * ATTN_WIDTH
FILTER_EMB_DIM = 33
FILTER_ORDER = 64
FILTER_TARGET = 1e-2
FILTER_FAST_DECAY_PCT = 0.3
FILTER_SLOW_DECAY_PCT = 1.5
D_FF = 4 * D_MODEL
ROPE_THETA = 10000.0
Q_BLOCK = 128
NORM_EPS = 1e-6
SUBLN_EPS = 1e-5

kernel_name = 'hybrid_hyena_diffattn_encoder_block'

F32 = jnp.float32


def rms_norm(x, gain, eps=NORM_EPS):
    xf = x.astype(F32)
    y = xf * lax.rsqrt(jnp.mean(xf * xf, axis=-1, keepdims=True) + eps)
    return (y * gain.astype(F32)).astype(x.dtype)


def short_conv_centred(u, w, b):
    L = u.shape[1]
    up = jnp.pad(u, ((0, 0), (1, 1), (0, 0)))
    return up[:, 0:L] * w[0] + up[:, 1:L + 1] * w[1] + up[:, 2:L + 2] * w[2] + b


def hyena_positional_features(L):
    bands = (FILTER_EMB_DIM - 1) // 2
    t = jnp.linspace(0.0, 1.0, L, dtype=F32)[:, None]
    w = (2.0 * math.pi / L) * jnp.arange(L, dtype=F32)[:, None]
    f = jnp.linspace(1e-4, bands - 1, bands, dtype=F32)[None, :]
    fw = f * w
    return jnp.concatenate([t, jnp.cos(fw), -jnp.sin(fw)], axis=-1)


def hyena_decay_window(L):
    t = jnp.linspace(0.0, 1.0, L, dtype=F32)[:, None]
    max_decay = math.log(FILTER_TARGET) / FILTER_FAST_DECAY_PCT
    min_decay = math.log(FILTER_TARGET) / FILTER_SLOW_DECAY_PCT
    deltas = jnp.linspace(min_decay, max_decay, HYENA_WIDTH, dtype=F32)[None, :]
    return jnp.exp(-t * jnp.abs(deltas))


def implicit_filters(z, decay, w1, b1, w2, b2, w3, b3, w4, freq):
    freq = freq.astype(F32)
    h = jnp.sin(freq * (z @ w1.astype(F32) + b1.astype(F32)))
    h = jnp.sin(freq * (h @ w2.astype(F32) + b2.astype(F32)))
    h = jnp.sin(freq * (h @ w3.astype(F32) + b3.astype(F32)))
    h = (h @ w4.astype(F32)).reshape(z.shape[0], 2, HYENA_WIDTH)
    h = h * decay[:, None, :]
    return h[:, 0], h[:, 1]


def fft_conv(u, h):
    L = u.shape[1]
    n = 2 * L
    uf = jnp.fft.rfft(u, n=n, axis=1)
    hf = jnp.fft.rfft(h, n=n, axis=0)
    return jnp.fft.irfft(uf * hf[None], n=n, axis=1)[:, :L]


def bidirectional_long_conv(u, h_fwd, h_bwd, bias):
    u32 = u.astype(F32)
    y_fwd = fft_conv(u32, h_fwd)
    y_bwd = jnp.flip(fft_conv(jnp.flip(u32, axis=1), h_bwd), axis=1)
    return y_fwd + y_bwd + u32 * bias.astype(F32)


def hyena_mixer(u, conv_w, conv_b, h_fwd, h_bwd, filt_bias):
    u = short_conv_centred(u, conv_w, conv_b)
    x0, x1, v = jnp.split(u, 3, axis=-1)
    y = bidirectional_long_conv(v * x1, h_fwd, h_bwd, filt_bias)
    return (y * x0.astype(F32)).astype(u.dtype)


def rope_tables(L):
    inv = ROPE_THETA ** (-jnp.arange(0, ATTN_HEAD_DIM, 2, dtype=F32) / ATTN_HEAD_DIM)
    ang = jnp.arange(L, dtype=F32)[:, None] * inv[None, :]
    ang = jnp.concatenate([ang, ang], axis=-1)
    return jnp.cos(ang), jnp.sin(ang)


def apply_rope(x, cos, sin):
    x = x.astype(F32)
    x1, x2 = jnp.split(x, 2, axis=-1)
    rot = jnp.concatenate([-x2, x1], axis=-1)
    return x * cos[None, :, None, None, :] + rot * sin[None, :, None, None, :]


def differential_attention(q, k, v, lam):
    B, S = q.shape[0], q.shape[1]
    nb = S // Q_BLOCK
    qb = q.reshape(B, nb, Q_BLOCK, ATTN_HEADS, 2, ATTN_HEAD_DIM).transpose(1, 0, 3, 4, 2, 5)
    kt = k.transpose(0, 2, 3, 1, 4)
    vt = v.astype(F32).transpose(0, 2, 1, 3)

    def block(q_blk):
        s = jnp.einsum('bhcqd,bhcsd->bhcqs', q_blk, kt)
        p = jax.nn.softmax(s, axis=-1)
        w = p[:, :, 0] - lam * p[:, :, 1]
        return jnp.einsum('bhqs,bhse->bhqe', w, vt)

    o = lax.map(block, qb)
    return o.transpose(1, 0, 3, 2, 4).reshape(B, S, ATTN_HEADS, 2 * ATTN_HEAD_DIM)


def setup_inputs(seed: int = 0) -> dict:
    key = jax.random.key(seed)
    ks = jax.random.split(key, 32)

    def nrm(k, shape, std):
        return std * jax.random.normal(k, shape, F32)

    def gain(k, n):
        return 1.0 + nrm(k, (DEPTH, n), 0.01)

    return {
        'x': jax.random.normal(ks[0], (BATCH, SEQ, D_MODEL), F32),
        'attn_pre_gain': gain(ks[1], D_MODEL),
        'attn_post_gain': gain(ks[2], D_MODEL),
        'w_in': nrm(ks[3], (DEPTH, D_MODEL, IN_WIDTH), D_MODEL ** -0.5),
        'conv_w': nrm(ks[4], (DEPTH, 3, 3 * HYENA_WIDTH), 3 ** -0.5),
        'conv_b': nrm(ks[5], (DEPTH, 3 * HYENA_WIDTH), 0.02),
        'filt_w1': nrm(ks[6], (DEPTH, FILTER_EMB_DIM, FILTER_ORDER), FILTER_EMB_DIM ** -0.5),
        'filt_b1': nrm(ks[7], (DEPTH, FILTER_ORDER), 0.1),
        'filt_w2': nrm(ks[8], (DEPTH, FILTER_ORDER, FILTER_ORDER), FILTER_ORDER ** -0.5),
        'filt_b2': nrm(ks[9], (DEPTH, FILTER_ORDER), 0.1),
        'filt_w3': nrm(ks[10], (DEPTH, FILTER_ORDER, FILTER_ORDER), FILTER_ORDER ** -0.5),
        'filt_b3': nrm(ks[11], (DEPTH, FILTER_ORDER), 0.1),
        'filt_w4': nrm(ks[12], (DEPTH, FILTER_ORDER, 2 * HYENA_WIDTH), 0.02),
        'filt_freq': 1.0 + nrm(ks[13], (DEPTH, FILTER_ORDER), 0.01),
        'filt_bias': nrm(ks[14], (DEPTH, HYENA_WIDTH), 1.0),
        'lam_q1': nrm(ks[15], (DEPTH, ATTN_HEAD_DIM), 0.1),
        'lam_k1': nrm(ks[16], (DEPTH, ATTN_HEAD_DIM), 0.1),
        'lam_q2': nrm(ks[17], (DEPTH, ATTN_HEAD_DIM), 0.1),
        'lam_k2': nrm(ks[18], (DEPTH, ATTN_HEAD_DIM), 0.1),
        'subln_gain': gain(ks[19], 2 * ATTN_HEAD_DIM),
        'w_out': nrm(ks[20], (DEPTH, HYENA_WIDTH + ATTN_WIDTH, D_MODEL), (HYENA_WIDTH + ATTN_WIDTH) ** -0.5),
        'mlp_pre_gain': gain(ks[21], D_MODEL),
        'mlp_post_gain': gain(ks[22], D_MODEL),
        'w_up': nrm(ks[23], (DEPTH, D_MODEL, D_FF), D_MODEL ** -0.5),
        'w_down': nrm(ks[24], (DEPTH, D_FF, D_MODEL), D_FF ** -0.5),
    }


def reference(x, attn_pre_gain, attn_post_gain, w_in, conv_w, conv_b, filt_w1, filt_b1, filt_w2, filt_b2,
              filt_w3, filt_b3, filt_w4, filt_freq, filt_bias, lam_q1, lam_k1, lam_q2, lam_k2, subln_gain,
              w_out, mlp_pre_gain, mlp_post_gain, w_up, w_down):
    B, S, _ = x.shape
    z = hyena_positional_features(S)
    decay = hyena_decay_window(S)
    cos, sin = rope_tables(S)
    scale = ATTN_HEAD_DIM ** -0.5
    for l in range(DEPTH):
        h = rms_norm(x, attn_pre_gain[l])
        proj = h @ w_in[l]
        u_hy = proj[..., :3 * HYENA_WIDTH]
        q, k, v = jnp.split(proj[..., 3 * HYENA_WIDTH:], 3, axis=-1)

        h_fwd, h_bwd = implicit_filters(z, decay, filt_w1[l], filt_b1[l], filt_w2[l], filt_b2[l],
                                        filt_w3[l], filt_b3[l], filt_w4[l], filt_freq[l])
        y_hy = hyena_mixer(u_hy, conv_w[l], conv_b[l], h_fwd, h_bwd, filt_bias[l])

        q = apply_rope(q.reshape(B, S, ATTN_HEADS, 2, ATTN_HEAD_DIM), cos, sin) * scale
        k = apply_rope(k.reshape(B, S, ATTN_HEADS, 2, ATTN_HEAD_DIM), cos, sin)
        v = v.reshape(B, S, ATTN_HEADS, 2 * ATTN_HEAD_DIM)
        lam_init = 0.8 - 0.6 * math.exp(-0.3 * l)
        lam = (jnp.exp(jnp.sum(lam_q1[l].astype(F32) * lam_k1[l].astype(F32)))
               - jnp.exp(jnp.sum(lam_q2[l].astype(F32) * lam_k2[l].astype(F32))) + lam_init)
        o = differential_attention(q, k, v, lam)
        o = rms_norm(o, subln_gain[l], SUBLN_EPS) * (1.0 - lam_init)
        y_attn = o.reshape(B, S, ATTN_WIDTH).astype(x.dtype)

        mix = jnp.concatenate([y_hy, y_attn], axis=-1) @ w_out[l]
        x = x + rms_norm(mix, attn_post_gain[l])

        h = rms_norm(x, mlp_pre_gain[l])
        y = jnp.square(jax.nn.relu(h @ w_up[l])) @ w_down[l]
        x = x + rms_norm(y, mlp_post_gain[l])
    return x
```

```python
import functools
import math

import jax
import jax.numpy as jnp
from jax import lax
from jax.experimental import pallas as pl
from jax.experimental.pallas import tpu as pltpu

F32 = jnp.float32
BF16 = jnp.bfloat16

HEAD_DIM = 64
HEAD_WIDTH = 2 * HEAD_DIM
FILTER_EMB_DIM = 33
FILTER_TARGET = 1e-2
FILTER_FAST_DECAY_PCT = 0.3
FILTER_SLOW_DECAY_PCT = 1.5
ROPE_THETA = 10000.0
NORM_EPS = 1e-6
SUBLN_EPS = 1e-5

LANES = 128
BF16_SUBLANES = 16
MIB = 1024 * 1024


def _cparams(semantics, vmem_mib):
    return pltpu.CompilerParams(dimension_semantics=semantics, vmem_limit_bytes=vmem_mib * MIB)


def _rms(x, eps):
    return x * lax.rsqrt(jnp.mean(x * x, axis=-1, keepdims=True) + eps)


def _inproj_kernel(x_ref, g_ref, w_ref, cos_ref, sa_ref, sb_ref, u_ref, q_ref, k_ref, v_ref, *, hy3, aw, scale):
    x = x_ref[...]
    hb = (_rms(x, NORM_EPS) * g_ref[...]).astype(BF16)
    u_ref[...] = jnp.dot(hb, w_ref[:, :hy3], preferred_element_type=F32).astype(BF16)
    cos, sa, sb = cos_ref[...], sa_ref[...], sb_ref[...]

    def rope(t):
        return t * cos + pltpu.roll(t, LANES - HEAD_DIM // 2, 1) * sa + pltpu.roll(t, HEAD_DIM // 2, 1) * sb

    q = jnp.dot(hb, w_ref[:, hy3:hy3 + aw], preferred_element_type=F32)
    k = jnp.dot(hb, w_ref[:, hy3 + aw:hy3 + 2 * aw], preferred_element_type=F32)
    for j in range(aw // LANES):
        sl = slice(j * LANES, (j + 1) * LANES)
        q_ref[:, sl] = (rope(q[:, sl]) * scale).astype(BF16)
        k_ref[:, sl] = rope(k[:, sl]).astype(BF16)
    v_ref[...] = jnp.dot(hb, w_ref[:, hy3 + 2 * aw:], preferred_element_type=F32).astype(BF16)


def _rope_tables(seq):
    inv = ROPE_THETA ** (-jnp.arange(0, HEAD_DIM, 2, dtype=F32) / HEAD_DIM)
    ang = jnp.arange(seq, dtype=F32)[:, None] * inv[None, :]
    ang = jnp.tile(ang, (1, LANES // (HEAD_DIM // 2)))
    first = (jnp.arange(LANES) % HEAD_DIM) < HEAD_DIM // 2
    sin = jnp.sin(ang)
    return jnp.cos(ang), jnp.where(first, -sin, 0.0), jnp.where(first, 0.0, sin)


def _inproj(x2, gain, w_bf, seq, hy3, aw, tm):
    t, d = x2.shape
    cos, sa, sb = _rope_tables(seq)
    nseq = seq // tm
    tab = pl.BlockSpec((tm, LANES), lambda i: (i % nseq, 0))
    row = lambda n: pl.BlockSpec((tm, n), lambda i: (i, 0))
    return pl.pallas_call(
        functools.partial(_inproj_kernel, hy3=hy3, aw=aw, scale=HEAD_DIM ** -0.5),
        grid=(t // tm,),
        in_specs=[row(d), pl.BlockSpec((1, d), lambda i: (0, 0)),
                  pl.BlockSpec(w_bf.shape, lambda i: (0, 0)), tab, tab, tab],
        out_specs=[row(hy3), row(aw), row(aw), row(aw)],
        out_shape=[jax.ShapeDtypeStruct((t, hy3), BF16)] + [jax.ShapeDtypeStruct((t, aw), BF16)] * 3,
        compiler_params=_cparams(("parallel",), 48),
        name="inproj",
    )(x2, gain, w_bf, cos, sa, sb)


def _hygate_kernel(prev_ref, cur_ref, next_ref, w_ref, b_ref, g_ref, x0_ref, *, c):
    i = pl.program_id(1)
    last = pl.num_programs(1) - 1
    u = cur_ref[0].astype(F32)
    tl = u.shape[0]
    prev_row = prev_ref[0, BF16_SUBLANES - 1:BF16_SUBLANES, :].astype(F32) * jnp.where(i > 0, 1.0, 0.0)
    next_row = next_ref[0, 0:1, :].astype(F32) * jnp.where(i < last, 1.0, 0.0)
    rows = lax.broadcasted_iota(jnp.int32, u.shape, 0)
    up = jnp.where(rows == 0, prev_row, pltpu.roll(u, 1, 0))
    dn = jnp.where(rows == tl - 1, next_row, pltpu.roll(u, tl - 1, 0))
    y = up * w_ref[0:1, :] + u * w_ref[1:2, :] + dn * w_ref[2:3, :] + b_ref[...]
    x0_ref[0] = y[:, :c].astype(BF16)
    g_ref[0] = (y[:, 2 * c:] * y[:, c:2 * c]).astype(BF16)


def _hygate(u, conv_w, conv_b, tl):
    b, s, c3 = u.shape
    c = c3 // 3
    nb = tl // BF16_SUBLANES
    nhalo = s // BF16_SUBLANES
    return pl.pallas_call(
        functools.partial(_hygate_kernel, c=c),
        grid=(b, s // tl),
        in_specs=[
            pl.BlockSpec((1, BF16_SUBLANES, c3), lambda bi, i: (bi, jnp.maximum(i * nb - 1, 0), 0)),
            pl.BlockSpec((1, tl, c3), lambda bi, i: (bi, i, 0)),
            pl.BlockSpec((1, BF16_SUBLANES, c3), lambda bi, i: (bi, jnp.minimum((i + 1) * nb, nhalo - 1), 0)),
            pl.BlockSpec((3, c3), lambda bi, i: (0, 0)),
            pl.BlockSpec((1, c3), lambda bi, i: (0, 0)),
        ],
        out_specs=[pl.BlockSpec((1, tl, c), lambda bi, i: (bi, i, 0))] * 2,
        out_shape=[jax.ShapeDtypeStruct((b, s, c), BF16)] * 2,
        compiler_params=_cparams(("parallel", "parallel"), 48),
        name="hygate",
    )(u, u, u, conv_w, conv_b)


def _filter_kernel(z_ref, dec_ref, w1_ref, b1_ref, w2_ref, b2_ref, w3_ref, b3_ref, w4_ref, fr_ref, o_ref, *, c):
    dot = functools.partial(jnp.dot, preferred_element_type=F32, precision=lax.Precision.HIGHEST)
    fr = fr_ref[...]
    h = jnp.sin(fr * (dot(z_ref[...], w1_ref[...]) + b1_ref[...]))
    h = jnp.sin(fr * (dot(h, w2_ref[...]) + b2_ref[...]))
    h = jnp.sin(fr * (dot(h, w3_ref[...]) + b3_ref[...]))
    h = dot(h, w4_ref[...])
    dec = dec_ref[...]
    o_ref[0] = h[:, :c] * dec
    o_ref[1] = h[:, c:] * dec


def _filter_inputs(seq, c):
    bands = (FILTER_EMB_DIM - 1) // 2
    t = jnp.linspace(0.0, 1.0, seq, dtype=F32)[:, None]
    w = (2.0 * math.pi / seq) * jnp.arange(seq, dtype=F32)[:, None]
    f = jnp.linspace(1e-4, bands - 1, bands, dtype=F32)[None, :]
    fw = f * w
    z = jnp.concatenate([t, jnp.cos(fw), -jnp.sin(fw)], axis=-1)
    max_decay = math.log(FILTER_TARGET) / FILTER_FAST_DECAY_PCT
    min_decay = math.log(FILTER_TARGET) / FILTER_SLOW_DECAY_PCT
    deltas = jnp.linspace(min_decay, max_decay, c, dtype=F32)[None, :]
    return z, jnp.exp(-t * jnp.abs(deltas))


def _filters(seq, c, w1, b1, w2, b2, w3, b3, w4, freq, ts):
    z, decay = _filter_inputs(seq, c)
    order = w1.shape[1]
    zp = jnp.pad(z, ((0, 0), (0, order - FILTER_EMB_DIM)))
    w1p = jnp.pad(w1, ((0, order - FILTER_EMB_DIM), (0, 0)))
    full = lambda a: pl.BlockSpec(a.shape, lambda i: (0,) * a.ndim)
    args = (zp, decay, w1p, b1, w2, b2, w3, b3, w4, freq)
    return pl.pallas_call(
        functools.partial(_filter_kernel, c=c),
        grid=(seq // ts,),
        in_specs=[pl.BlockSpec((ts, order), lambda i: (i, 0)), pl.BlockSpec((ts, c), lambda i: (i, 0))]
        + [full(a) for a in args[2:]],
        out_specs=pl.BlockSpec((2, ts, c), lambda i: (0, i, 0)),
        out_shape=jax.ShapeDtypeStruct((2, seq, c), F32),
        compiler_params=_cparams(("parallel",), 48),
        name="hyfilter",
    )(*args)


def _cis(m, n):
    ang = (m % n).astype(F32) * (2.0 * math.pi / n)
    return jnp.cos(ang), jnp.sin(ang)


def _dft_tables(n1, n2):
    n = n1 * n2
    n1h = n1 // 2
    k1 = jnp.arange(n1, dtype=jnp.int32)[:, None]
    j1 = jnp.arange(n1h, dtype=jnp.int32)[None, :]
    c, s = _cis(k1 * j1, n1)
    fa = jnp.stack([c, -s], axis=1).reshape(2 * n1, n1h)
    fb = jnp.stack([s, c], axis=1).reshape(2 * n1, n1h)
    f1 = jnp.concatenate([fa, fb], axis=1)
    kk = (jnp.arange(n1, dtype=jnp.int32)[:, None, None] + n1 * jnp.arange(n2, dtype=jnp.int32)[None, :, None])
    j2 = jnp.arange(n2, dtype=jnp.int32)[None, None, :]
    c, s = _cis(kk * j2, n)
    m2 = jnp.concatenate([jnp.concatenate([c, s], axis=2), jnp.concatenate([-s, c], axis=2)], axis=1)
    ct, st = jnp.swapaxes(c, 1, 2), jnp.swapaxes(s, 1, 2)
    m2i = jnp.concatenate([jnp.concatenate([ct, -st], axis=2), jnp.concatenate([st, ct], axis=2)], axis=1)
    j1 = jnp.arange(n1h, dtype=jnp.int32)[:, None]
    k1 = jnp.arange(n1, dtype=jnp.int32)[None, :]
    c, s = _cis(j1 * k1, n1)
    c, s = c / n, s / n
    g_re = jnp.stack([c, -s], axis=2).reshape(n1h, 2 * n1)
    g_im = jnp.stack([s, c], axis=2).reshape(n1h, 2 * n1)
    g1 = jnp.concatenate([g_re, g_im], axis=0)
    return f1.astype(BF16), m2.astype(BF16), m2i.astype(BF16), g1.astype(BF16)


def _fft1_kernel(f_ref, zr_ref, zi_ref, o_ref):
    z = jnp.concatenate([zr_ref[0].astype(BF16), zi_ref[0].astype(BF16)], axis=0)
    o_ref[0] = jnp.dot(f_ref[...], z, preferred_element_type=F32).astype(BF16)


def _fft1_real_kernel(f_ref, zr_ref, o_ref):
    o_ref[0] = jnp.dot(f_ref[...], zr_ref[0].astype(BF16), preferred_element_type=F32).astype(BF16)


def _fft1(f1, z, pairs, nt, paired):
    _, n1h, nn = z.shape
    rows = f1.shape[0]
    blk = lambda off: pl.BlockSpec((1, n1h, nt), lambda p, t: (p + off, 0, t))
    if paired:
        kern, ins, args = _fft1_kernel, [pl.BlockSpec(f1.shape, lambda p, t: (0, 0)), blk(0), blk(pairs)], (f1, z, z)
    else:
        fa = f1[:, :n1h]
        kern, ins, args = _fft1_real_kernel, [pl.BlockSpec(fa.shape, lambda p, t: (0, 0)), blk(0)], (fa, z)
    return pl.pallas_call(
        kern,
        grid=(pairs, nn // nt),
        in_specs=ins,
        out_specs=pl.BlockSpec((1, rows, nt), lambda p, t: (p, 0, t)),
        out_shape=jax.ShapeDtypeStruct((pairs, rows, nn), BF16),
        compiler_params=_cparams(("parallel", "parallel"), 48),
        name="fft1",
    )(*args)


def _spectrum_kernel(m2_ref, a_ref, bias_ref, h_ref, *, kb, n2):
    bias = bias_ref[...]
    for i in range(kb):
        xf = jnp.dot(m2_ref[i], a_ref[0, i], preferred_element_type=F32)
        xb = jnp.dot(m2_ref[i], a_ref[1, i], preferred_element_type=F32)
        h_ref[i, :n2, :] = xf[:n2] + xb[:n2] + bias
        h_ref[i, n2:, :] = xf[n2:] - xb[n2:]


def _spectrum(m2, af, bias, kb):
    _, n1, r, c = af.shape
    return pl.pallas_call(
        functools.partial(_spectrum_kernel, kb=kb, n2=r // 2),
        grid=(n1 // kb,),
        in_specs=[pl.BlockSpec((kb, r, r), lambda k: (k, 0, 0)),
                  pl.BlockSpec((2, kb, r, c), lambda k: (0, k, 0, 0)),
                  pl.BlockSpec((1, c), lambda k: (0, 0))],
        out_specs=pl.BlockSpec((kb, r, c), lambda k: (k, 0, 0)),
        out_shape=jax.ShapeDtypeStruct((n1, r, c), F32),
        compiler_params=_cparams(("parallel",), 48),
        name="hyspectrum",
    )(m2, af, bias)


def _fft2_kernel(m2_ref, m2i_ref, h_ref, a_ref, o_ref, *, kb, n2):
    for i in range(kb):
        x = jnp.dot(m2_ref[i], a_ref[0, i], preferred_element_type=F32)
        xr, xi = x[:n2], x[n2:]
        hr, hi = h_ref[i, :n2, :], h_ref[i, n2:, :]
        y = jnp.concatenate([xr * hr - xi * hi, xr * hi + xi * hr], axis=0).astype(BF16)
        o_ref[0, i] = jnp.dot(m2i_ref[i], y, preferred_element_type=F32).astype(BF16)


def _fft2(m2, m2i, h, a, kb):
    pairs, n1, r, c = a.shape
    mat = pl.BlockSpec((kb, r, r), lambda k, p: (k, 0, 0))
    return pl.pallas_call(
        functools.partial(_fft2_kernel, kb=kb, n2=r // 2),
        grid=(n1 // kb, pairs),
        in_specs=[mat, mat, pl.BlockSpec((kb, r, c), lambda k, p: (k, 0, 0)),
                  pl.BlockSpec((1, kb, r, c), lambda k, p: (p, k, 0, 0))],
        out_specs=pl.BlockSpec((1, kb, r, c), lambda k, p: (p, k, 0, 0)),
        out_shape=jax.ShapeDtypeStruct(a.shape, BF16),
        compiler_params=_cparams(("parallel", "parallel"), 48),
        name="fft2",
    )(m2, m2i, h, a)


def _ifft1_kernel(g_ref, b_ref, xa_ref, xb_ref, o_ref, *, n1h):
    y = jnp.dot(g_ref[...], b_ref[0], preferred_element_type=F32)
    o_ref[0, 0] = (y[:n1h] * xa_ref[0].astype(F32)).astype(BF16)
    o_ref[1, 0] = (y[n1h:] * xb_ref[0].astype(F32)).astype(BF16)


def _ifft1(g1, bp, x0, nt):
    pairs, rows, nn = bp.shape
    n1h = g1.shape[0] // 2
    xblk = lambda off: pl.BlockSpec((1, n1h, nt), lambda p, t: (p + off, 0, t))
    return pl.pallas_call(
        functools.partial(_ifft1_kernel, n1h=n1h),
        grid=(pairs, nn // nt),
        in_specs=[pl.BlockSpec(g1.shape, lambda p, t: (0, 0)),
                  pl.BlockSpec((1, rows, nt), lambda p, t: (p, 0, t)), xblk(0), xblk(pairs)],
        out_specs=pl.BlockSpec((2, 1, n1h, nt), lambda p, t: (0, p, 0, t)),
        out_shape=jax.ShapeDtypeStruct((2, pairs, n1h, nn), BF16),
        compiler_params=_cparams(("parallel", "parallel"), 48),
        name="ifft1",
    )(g1, bp, x0, x0)


def _fft_split(seq):
    n1 = min(128, seq // 8)
    return n1, 2 * seq // n1


def _long_conv_gated(g, x0, hfb, filt_bias):
    b, seq, c = g.shape
    pairs = b // 2
    n1, n2 = _fft_split(seq)
    n1h = n1 // 2
    f1, m2, m2i, g1 = _dft_tables(n1, n2)
    nn = n2 * c
    nt = _tile(nn, 4096)
    kb = _tile(n1, 8)
    af = _fft1(f1, hfb.reshape(2, n1h, nn), 2, nt, paired=False).reshape(2, n1, 2 * n2, c)
    h = _spectrum(m2, af, filt_bias, kb)
    a = _fft1(f1, g.reshape(b, n1h, nn), pairs, nt, paired=True).reshape(pairs, n1, 2 * n2, c)
    bp = _fft2(m2, m2i, h, a, kb).reshape(pairs, 2 * n1, nn)
    y = _ifft1(g1, bp, x0.reshape(b, n1h, nn), nt)
    return y.reshape(b, seq, c)


def _attn_kernel(lq1_ref, lk1_ref, lq2_ref, lk2_ref, sg_ref, q_ref, k_ref, v_ref, o_ref, *, lam_init):
    lam = (jnp.exp(jnp.sum(lq1_ref[...] * lk1_ref[...], axis=-1, keepdims=True))
           - jnp.exp(jnp.sum(lq2_ref[...] * lk2_ref[...], axis=-1, keepdims=True)) + lam_init)
    q = q_ref[0]
    k = k_ref[0]
    lane = lax.broadcasted_iota(jnp.int32, q.shape, 1)
    zero = jnp.zeros_like(q)

    def softmax_parts(qc):
        s = lax.dot_general(qc, k, (((1,), (1,)), ((), ())), preferred_element_type=F32)
        p = jnp.exp(s - jnp.max(s, axis=-1, keepdims=True))
        return p, jnp.sum(p, axis=-1, keepdims=True)

    p1, l1 = softmax_parts(jnp.where(lane < HEAD_DIM, q, zero))
    p2, l2 = softmax_parts(jnp.where(lane < HEAD_DIM, zero, q))
    w = p1 * (1.0 / l1) - p2 * (lam / l2)
    o = jnp.dot(w.astype(BF16), v_ref[0], preferred_element_type=F32)
    o = _rms(o, SUBLN_EPS) * sg_ref[...] * (1.0 - lam_init)
    o_ref[0] = o.astype(BF16)


def _attention(q, k, v, lq1, lk1, lq2, lk2, subln_gain, lam_init, qb):
    b, s, aw = q.shape
    heads = aw // HEAD_WIDTH
    vec = lambda a: pl.BlockSpec(a.shape, lambda bi, h, i: (0, 0))
    kv = pl.BlockSpec((1, s, HEAD_WIDTH), lambda bi, h, i: (bi, 0, h))
    qo = pl.BlockSpec((1, qb, HEAD_WIDTH), lambda bi, h, i: (bi, i, h))
    return pl.pallas_call(
        functools.partial(_attn_kernel, lam_init=lam_init),
        grid=(b, heads, s // qb),
        in_specs=[vec(lq1), vec(lk1), vec(lq2), vec(lk2), vec(subln_gain), qo, kv, kv],
        out_specs=qo,
        out_shape=jax.ShapeDtypeStruct((b, s, aw), BF16),
        compiler_params=_cparams(("parallel", "parallel", "parallel"), 56),
        name="diffattn",
    )(lq1, lk1, lq2, lk2, subln_gain, q, k, v)


def _outproj_kernel(yh_ref, ya_ref, w_ref, g_ref, x_ref, o_ref, *, c):
    mix = (jnp.dot(yh_ref[...], w_ref[:c, :], preferred_element_type=F32)
           + jnp.dot(ya_ref[...], w_ref[c:, :], preferred_element_type=F32))
    o_ref[...] = x_ref[...] + _rms(mix, NORM_EPS) * g_ref[...]


def _outproj(yh, ya, w_bf, gain, x2, tm):
    t, d = x2.shape
    c = yh.shape[1]
    row = lambda n: pl.BlockSpec((tm, n), lambda i: (i, 0))
    return pl.pallas_call(
        functools.partial(_outproj_kernel, c=c),
        grid=(t // tm,),
        in_specs=[row(c), row(ya.shape[1]), pl.BlockSpec(w_bf.shape, lambda i: (0, 0)),
                  pl.BlockSpec((1, d), lambda i: (0, 0)), row(d)],
        out_specs=row(d),
        out_shape=jax.ShapeDtypeStruct((t, d), F32),
        compiler_params=_cparams(("parallel",), 48),
        name="outproj",
    )(yh, ya, w_bf, gain, x2)


def _mlp_kernel(x_ref, gpre_ref, wu_ref, wd_ref, gpost_ref, o_ref, *, fc):
    x = x_ref[...]
    hb = (_rms(x, NORM_EPS) * gpre_ref[...]).astype(BF16)
    acc = jnp.zeros(x.shape, F32)
    for j in range(wu_ref.shape[1] // fc):
        up = jnp.dot(hb, wu_ref[:, j * fc:(j + 1) * fc], preferred_element_type=F32)
        act = jnp.square(jnp.maximum(up, 0.0)).astype(BF16)
        acc = acc + jnp.dot(act, wd_ref[j * fc:(j + 1) * fc, :], preferred_element_type=F32)
    o_ref[...] = x + _rms(acc, NORM_EPS) * gpost_ref[...]


def _mlp(x2, gpre, wu_bf, wd_bf, gpost, tm, fc):
    t, d = x2.shape
    row = pl.BlockSpec((tm, d), lambda i: (i, 0))
    vec = pl.BlockSpec((1, d), lambda i: (0, 0))
    resident = lambda a: pl.BlockSpec(a.shape, lambda i: (0, 0), pipeline_mode=pl.Buffered(1))
    return pl.pallas_call(
        functools.partial(_mlp_kernel, fc=fc),
        grid=(t // tm,),
        in_specs=[row, vec, resident(wu_bf), resident(wd_bf), vec],
        out_specs=row,
        out_shape=jax.ShapeDtypeStruct((t, d), F32),
        compiler_params=_cparams(("parallel",), 56),
        name="mlp",
    )(x2, gpre, wu_bf, wd_bf, gpost)


def _tile(n, want):
    return min(n, want)


def kernel(x, attn_pre_gain, attn_post_gain, w_in, conv_w, conv_b, filt_w1, filt_b1, filt_w2, filt_b2, filt_w3, filt_b3, filt_w4, filt_freq, filt_bias, lam_q1, lam_k1, lam_q2, lam_k2, subln_gain, w_out, mlp_pre_gain, mlp_post_gain, w_up, w_down):
    b, s, d = x.shape
    depth = w_in.shape[0]
    c = filt_bias.shape[1]
    hy3 = 3 * c
    aw = (w_in.shape[2] - hy3) // 3
    t = b * s
    tm = _tile(t, 512)
    row = lambda a, l: a[l][None, :]
    x2 = x.reshape(t, d)
    for l in range(depth):
        u, q, k, v = _inproj(x2, row(attn_pre_gain, l), w_in[l].astype(BF16), s, hy3, aw, _tile(s, 512))
        g, x0 = _hygate(u.reshape(b, s, hy3), conv_w[l], row(conv_b, l), _tile(s, 512))
        hfb = _filters(s, c, filt_w1[l], row(filt_b1, l), filt_w2[l], row(filt_b2, l), filt_w3[l],
                       row(filt_b3, l), filt_w4[l], row(filt_freq, l), _tile(s, 512))
        y_hy = _long_conv_gated(g, x0, hfb, row(filt_bias, l))
        lam_init = 0.8 - 0.6 * math.exp(-0.3 * l)
        y_at = _attention(q.reshape(b, s, aw), k.reshape(b, s, aw), v.reshape(b, s, aw),
                          row(lam_q1, l), row(lam_k1, l), row(lam_q2, l), row(lam_k2, l),
                          row(subln_gain, l), lam_init, _tile(s, 256))
        x2 = _outproj(y_hy.reshape(t, c), y_at.reshape(t, aw), w_out[l].astype(BF16),
                      row(attn_post_gain, l), x2, tm)
        x2 = _mlp(x2, row(mlp_pre_gain, l), w_up[l].astype(BF16), w_down[l].astype(BF16),
                  row(mlp_post_gain, l), tm, _tile(w_up.shape[2], 1024))
    return x2.reshape(b, s, d)
```

```python
import functools
import math

import jax
import jax.numpy as jnp
from jax import lax
from jax.experimental import pallas as pl
from jax.experimental.pallas import tpu as pltpu

F32 = jnp.float32
BF16 = jnp.bfloat16

HEAD_DIM = 64
HEAD_WIDTH = 2 * HEAD_DIM
FILTER_EMB_DIM = 33
FILTER_TARGET = 1e-2
FILTER_FAST_DECAY_PCT = 0.3
FILTER_SLOW_DECAY_PCT = 1.5
ROPE_THETA = 10000.0
NORM_EPS = 1e-6
SUBLN_EPS = 1e-5

LANES = 128
BF16_SUBLANES = 16
MIB = 1024 * 1024


def _cparams(semantics, vmem_mib):
    return pltpu.CompilerParams(dimension_semantics=semantics, vmem_limit_bytes=vmem_mib * MIB)


def _rms(x, eps):
    return x * lax.rsqrt(jnp.mean(x * x, axis=-1, keepdims=True) + eps)


def _inproj_kernel(x_ref, g_ref, w_ref, cos_ref, sa_ref, sb_ref, u_ref, q_ref, k_ref, v_ref, *, hy3, aw, scale):
    x = x_ref[...]
    hb = (_rms(x, NORM_EPS) * g_ref[...]).astype(BF16)
    u_ref[...] = jnp.dot(hb, w_ref[:, :hy3], preferred_element_type=F32).astype(BF16)
    cos, sa, sb = cos_ref[...], sa_ref[...], sb_ref[...]

    def rope(t):
        return t * cos + pltpu.roll(t, LANES - HEAD_DIM // 2, 1) * sa + pltpu.roll(t, HEAD_DIM // 2, 1) * sb

    q = jnp.dot(hb, w_ref[:, hy3:hy3 + aw], preferred_element_type=F32)
    k = jnp.dot(hb, w_ref[:, hy3 + aw:hy3 + 2 * aw], preferred_element_type=F32)
    for j in range(aw // LANES):
        sl = slice(j * LANES, (j + 1) * LANES)
        q_ref[:, sl] = (rope(q[:, sl]) * scale).astype(BF16)
        k_ref[:, sl] = rope(k[:, sl]).astype(BF16)
    v_ref[...] = jnp.dot(hb, w_ref[:, hy3 + 2 * aw:], preferred_element_type=F32).astype(BF16)


def _rope_tables(seq):
    inv = ROPE_THETA ** (-jnp.arange(0, HEAD_DIM, 2, dtype=F32) / HEAD_DIM)
    ang = jnp.arange(seq, dtype=F32)[:, None] * inv[None, :]
    ang = jnp.tile(ang, (1, LANES // (HEAD_DIM // 2)))
    first = (jnp.arange(LANES) % HEAD_DIM) < HEAD_DIM // 2
    sin = jnp.sin(ang)
    return jnp.cos(ang), jnp.where(first, -sin, 0.0), jnp.where(first, 0.0, sin)


def _inproj(x2, gain, w_bf, seq, hy3, aw, tm):
    t, d = x2.shape
    cos, sa, sb = _rope_tables(seq)
    nseq = seq // tm
    tab = pl.BlockSpec((tm, LANES), lambda i: (i % nseq, 0))
    row = lambda n: pl.BlockSpec((tm, n), lambda i: (i, 0))
    return pl.pallas_call(
        functools.partial(_inproj_kernel, hy3=hy3, aw=aw, scale=HEAD_DIM ** -0.5 * math.log2(math.e)),
        grid=(t // tm,),
        in_specs=[row(d), pl.BlockSpec((1, d), lambda i: (0, 0)),
                  pl.BlockSpec(w_bf.shape, lambda i: (0, 0)), tab, tab, tab],
        out_specs=[row(hy3), row(aw), row(aw), row(aw)],
        out_shape=[jax.ShapeDtypeStruct((t, hy3), BF16)] + [jax.ShapeDtypeStruct((t, aw), BF16)] * 3,
        compiler_params=_cparams(("parallel",), 48),
        name="inproj",
    )(x2, gain, w_bf, cos, sa, sb)


def _hygate_kernel(prev_ref, cur_ref, next_ref, w_ref, b_ref, g_ref, x0_ref, *, c):
    i = pl.program_id(1)
    last = pl.num_programs(1) - 1
    u = cur_ref[0].astype(F32)
    tl = u.shape[0]
    prev_row = prev_ref[0, BF16_SUBLANES - 1:BF16_SUBLANES, :].astype(F32) * jnp.where(i > 0, 1.0, 0.0)
    next_row = next_ref[0, 0:1, :].astype(F32) * jnp.where(i < last, 1.0, 0.0)
    rows = lax.broadcasted_iota(jnp.int32, u.shape, 0)
    up = jnp.where(rows == 0, prev_row, pltpu.roll(u, 1, 0))
    dn = jnp.where(rows == tl - 1, next_row, pltpu.roll(u, tl - 1, 0))
    y = up * w_ref[0:1, :] + u * w_ref[1:2, :] + dn * w_ref[2:3, :] + b_ref[...]
    x0_ref[0] = y[:, :c].astype(BF16)
    g_ref[0] = (y[:, 2 * c:] * y[:, c:2 * c]).astype(BF16)


def _hygate(u, conv_w, conv_b, tl):
    b, s, c3 = u.shape
    c = c3 // 3
    nb = tl // BF16_SUBLANES
    nhalo = s // BF16_SUBLANES
    return pl.pallas_call(
        functools.partial(_hygate_kernel, c=c),
        grid=(b, s // tl),
        in_specs=[
            pl.BlockSpec((1, BF16_SUBLANES, c3), lambda bi, i: (bi, jnp.maximum(i * nb - 1, 0), 0)),
            pl.BlockSpec((1, tl, c3), lambda bi, i: (bi, i, 0)),
            pl.BlockSpec((1, BF16_SUBLANES, c3), lambda bi, i: (bi, jnp.minimum((i + 1) * nb, nhalo - 1), 0)),
            pl.BlockSpec((3, c3), lambda bi, i: (0, 0)),
            pl.BlockSpec((1, c3), lambda bi, i: (0, 0)),
        ],
        out_specs=[pl.BlockSpec((1, tl, c), lambda bi, i: (bi, i, 0))] * 2,
        out_shape=[jax.ShapeDtypeStruct((b, s, c), BF16)] * 2,
        compiler_params=_cparams(("parallel", "parallel"), 48),
        name="hygate",
    )(u, u, u, conv_w, conv_b)


def _filter_kernel(z_ref, dec_ref, w1_ref, b1_ref, w2_ref, b2_ref, w3_ref, b3_ref, w4_ref, fr_ref, o_ref, *, c):
    dot = functools.partial(jnp.dot, preferred_element_type=F32, precision=lax.Precision.HIGHEST)
    fr = fr_ref[...]
    h = jnp.sin(fr * (dot(z_ref[...], w1_ref[...]) + b1_ref[...]))
    h = jnp.sin(fr * (dot(h, w2_ref[...]) + b2_ref[...]))
    h = jnp.sin(fr * (dot(h, w3_ref[...]) + b3_ref[...]))
    h = dot(h, w4_ref[...])
    dec = dec_ref[...]
    o_ref[0] = h[:, :c] * dec
    o_ref[1] = h[:, c:] * dec


def _filter_inputs(seq, c):
    bands = (FILTER_EMB_DIM - 1) // 2
    t = jnp.linspace(0.0, 1.0, seq, dtype=F32)[:, None]
    w = (2.0 * math.pi / seq) * jnp.arange(seq, dtype=F32)[:, None]
    f = jnp.linspace(1e-4, bands - 1, bands, dtype=F32)[None, :]
    fw = f * w
    z = jnp.concatenate([t, jnp.cos(fw), -jnp.sin(fw)], axis=-1)
    max_decay = math.log(FILTER_TARGET) / FILTER_FAST_DECAY_PCT
    min_decay = math.log(FILTER_TARGET) / FILTER_SLOW_DECAY_PCT
    deltas = jnp.linspace(min_decay, max_decay, c, dtype=F32)[None, :]
    return z, jnp.exp(-t * jnp.abs(deltas))


def _filters(seq, c, w1, b1, w2, b2, w3, b3, w4, freq, ts):
    z, decay = _filter_inputs(seq, c)
    order = w1.shape[1]
    zp = jnp.pad(z, ((0, 0), (0, order - FILTER_EMB_DIM)))
    w1p = jnp.pad(w1, ((0, order - FILTER_EMB_DIM), (0, 0)))
    full = lambda a: pl.BlockSpec(a.shape, lambda i: (0,) * a.ndim)
    args = (zp, decay, w1p, b1, w2, b2, w3, b3, w4, freq)
    return pl.pallas_call(
        functools.partial(_filter_kernel, c=c),
        grid=(seq // ts,),
        in_specs=[pl.BlockSpec((ts, order), lambda i: (i, 0)), pl.BlockSpec((ts, c), lambda i: (i, 0))]
        + [full(a) for a in args[2:]],
        out_specs=pl.BlockSpec((2, ts, c), lambda i: (0, i, 0)),
        out_shape=jax.ShapeDtypeStruct((2, seq, c), F32),
        compiler_params=_cparams(("parallel",), 48),
        name="hyfilter",
    )(*args)


def _cis(m, n):
    ang = (m % n).astype(F32) * (2.0 * math.pi / n)
    return jnp.cos(ang), jnp.sin(ang)


def _dft_tables(n1, n2):
    n = n1 * n2
    n1h = n1 // 2
    k1 = jnp.arange(n1, dtype=jnp.int32)[:, None]
    j1 = jnp.arange(n1h, dtype=jnp.int32)[None, :]
    c, s = _cis(k1 * j1, n1)
    fa = jnp.stack([c, -s], axis=1).reshape(2 * n1, n1h)
    fb = jnp.stack([s, c], axis=1).reshape(2 * n1, n1h)
    f1 = jnp.concatenate([fa, fb], axis=1)
    kk = (jnp.arange(n1, dtype=jnp.int32)[:, None, None] + n1 * jnp.arange(n2, dtype=jnp.int32)[None, :, None])
    j2 = jnp.arange(n2, dtype=jnp.int32)[None, None, :]
    c, s = _cis(kk * j2, n)
    m2 = jnp.concatenate([jnp.concatenate([c, s], axis=2), jnp.concatenate([-s, c], axis=2)], axis=1)
    ct, st = jnp.swapaxes(c, 1, 2), jnp.swapaxes(s, 1, 2)
    m2i = jnp.concatenate([jnp.concatenate([ct, -st], axis=2), jnp.concatenate([st, ct], axis=2)], axis=1)
    j1 = jnp.arange(n1h, dtype=jnp.int32)[:, None]
    k1 = jnp.arange(n1, dtype=jnp.int32)[None, :]
    c, s = _cis(j1 * k1, n1)
    c, s = c / n, s / n
    g_re = jnp.stack([c, -s], axis=2).reshape(n1h, 2 * n1)
    g_im = jnp.stack([s, c], axis=2).reshape(n1h, 2 * n1)
    g1 = jnp.concatenate([g_re, g_im], axis=0)
    return f1.astype(BF16), m2.astype(BF16), m2i.astype(BF16), g1.astype(BF16)


def _fft1_kernel(f_ref, zr_ref, zi_ref, o_ref):
    z = jnp.concatenate([zr_ref[0].astype(BF16), zi_ref[0].astype(BF16)], axis=0)
    o_ref[0] = jnp.dot(f_ref[...], z, preferred_element_type=F32).astype(BF16)


def _fft1_real_kernel(f_ref, zr_ref, o_ref):
    o_ref[0] = jnp.dot(f_ref[...], zr_ref[0].astype(BF16), preferred_element_type=F32).astype(BF16)


def _fft1(f1, z, pairs, nt, paired):
    _, n1h, nn = z.shape
    rows = f1.shape[0]
    blk = lambda off: pl.BlockSpec((1, n1h, nt), lambda p, t: (p + off, 0, t))
    if paired:
        kern, ins, args = _fft1_kernel, [pl.BlockSpec(f1.shape, lambda p, t: (0, 0)), blk(0), blk(pairs)], (f1, z, z)
    else:
        fa = f1[:, :n1h]
        kern, ins, args = _fft1_real_kernel, [pl.BlockSpec(fa.shape, lambda p, t: (0, 0)), blk(0)], (fa, z)
    return pl.pallas_call(
        kern,
        grid=(pairs, nn // nt),
        in_specs=ins,
        out_specs=pl.BlockSpec((1, rows, nt), lambda p, t: (p, 0, t)),
        out_shape=jax.ShapeDtypeStruct((pairs, rows, nn), BF16),
        compiler_params=_cparams(("parallel", "parallel"), 48),
        name="fft1",
    )(*args)


def _spectrum_kernel(m2_ref, a_ref, bias_ref, h_ref, *, kb, n2):
    bias = bias_ref[...]
    for i in range(kb):
        xf = jnp.dot(m2_ref[i], a_ref[0, i], preferred_element_type=F32)
        xb = jnp.dot(m2_ref[i], a_ref[1, i], preferred_element_type=F32)
        h_ref[i, :n2, :] = xf[:n2] + xb[:n2] + bias
        h_ref[i, n2:, :] = xf[n2:] - xb[n2:]


def _spectrum(m2, af, bias, kb):
    _, n1, r, c = af.shape
    return pl.pallas_call(
        functools.partial(_spectrum_kernel, kb=kb, n2=r // 2),
        grid=(n1 // kb,),
        in_specs=[pl.BlockSpec((kb, r, r), lambda k: (k, 0, 0)),
                  pl.BlockSpec((2, kb, r, c), lambda k: (0, k, 0, 0)),
                  pl.BlockSpec((1, c), lambda k: (0, 0))],
        out_specs=pl.BlockSpec((kb, r, c), lambda k: (k, 0, 0)),
        out_shape=jax.ShapeDtypeStruct((n1, r, c), F32),
        compiler_params=_cparams(("parallel",), 48),
        name="hyspectrum",
    )(m2, af, bias)


def _fft2_kernel(m2_ref, m2i_ref, h_ref, a_ref, o_ref, *, kb, n2):
    for i in range(kb):
        x = jnp.dot(m2_ref[i], a_ref[0, i], preferred_element_type=F32)
        xr, xi = x[:n2], x[n2:]
        hr, hi = h_ref[i, :n2, :], h_ref[i, n2:, :]
        y = jnp.concatenate([xr * hr - xi * hi, xr * hi + xi * hr], axis=0).astype(BF16)
        o_ref[0, i] = jnp.dot(m2i_ref[i], y, preferred_element_type=F32).astype(BF16)


def _fft2(m2, m2i, h, a, kb):
    pairs, n1, r, c = a.shape
    mat = pl.BlockSpec((kb, r, r), lambda k, p: (k, 0, 0))
    return pl.pallas_call(
        functools.partial(_fft2_kernel, kb=kb, n2=r // 2),
        grid=(n1 // kb, pairs),
        in_specs=[mat, mat, pl.BlockSpec((kb, r, c), lambda k, p: (k, 0, 0)),
                  pl.BlockSpec((1, kb, r, c), lambda k, p: (p, k, 0, 0))],
        out_specs=pl.BlockSpec((1, kb, r, c), lambda k, p: (p, k, 0, 0)),
        out_shape=jax.ShapeDtypeStruct(a.shape, BF16),
        compiler_params=_cparams(("parallel", "parallel"), 48),
        name="fft2",
    )(m2, m2i, h, a)


def _ifft1_kernel(g_ref, b_ref, xa_ref, xb_ref, o_ref, *, n1h):
    y = jnp.dot(g_ref[...], b_ref[0], preferred_element_type=F32)
    o_ref[0, 0] = (y[:n1h] * xa_ref[0].astype(F32)).astype(BF16)
    o_ref[1, 0] = (y[n1h:] * xb_ref[0].astype(F32)).astype(BF16)


def _ifft1(g1, bp, x0, nt):
    pairs, rows, nn = bp.shape
    n1h = g1.shape[0] // 2
    xblk = lambda off: pl.BlockSpec((1, n1h, nt), lambda p, t: (p + off, 0, t))
    return pl.pallas_call(
        functools.partial(_ifft1_kernel, n1h=n1h),
        grid=(pairs, nn // nt),
        in_specs=[pl.BlockSpec(g1.shape, lambda p, t: (0, 0)),
                  pl.BlockSpec((1, rows, nt), lambda p, t: (p, 0, t)), xblk(0), xblk(pairs)],
        out_specs=pl.BlockSpec((2, 1, n1h, nt), lambda p, t: (0, p, 0, t)),
        out_shape=jax.ShapeDtypeStruct((2, pairs, n1h, nn), BF16),
        compiler_params=_cparams(("parallel", "parallel"), 48),
        name="ifft1",
    )(g1, bp, x0, x0)


def _fft_split(seq):
    n1 = min(128, seq // 8)
    return n1, 2 * seq // n1


def _long_conv_gated(g, x0, hfb, filt_bias):
    b, seq, c = g.shape
    pairs = b // 2
    n1, n2 = _fft_split(seq)
    n1h = n1 // 2
    f1, m2, m2i, g1 = _dft_tables(n1, n2)
    nn = n2 * c
    nt = _tile(nn, 4096)
    kb = _tile(n1, 8)
    af = _fft1(f1, hfb.reshape(2, n1h, nn), 2, nt, paired=False).reshape(2, n1, 2 * n2, c)
    h = _spectrum(m2, af, filt_bias, kb)
    a = _fft1(f1, g.reshape(b, n1h, nn), pairs, nt, paired=True).reshape(pairs, n1, 2 * n2, c)
    bp = _fft2(m2, m2i, h, a, kb).reshape(pairs, 2 * n1, nn)
    y = _ifft1(g1, bp, x0.reshape(b, n1h, nn), nt)
    return y.reshape(b, seq, c)


def _lane_fold(x, op):
    acc = x[:, :LANES]
    for t in range(1, x.shape[1] // LANES):
        acc = op(acc, x[:, t * LANES:(t + 1) * LANES])
    return acc


def _attn_kernel(lq1_ref, lk1_ref, lq2_ref, lk2_ref, sg_ref, q_ref, k_ref, v_ref, o_ref,
                 s_ref, p_ref, m_ref, l_ref, *, lam_init, qb, kc):
    seq = k_ref.shape[1]
    nk, nb = seq // kc, seq // qb
    lam = (jnp.exp(jnp.sum(lq1_ref[...] * lk1_ref[...], axis=-1, keepdims=True))
           - jnp.exp(jnp.sum(lq2_ref[...] * lk2_ref[...], axis=-1, keepdims=True)) + lam_init)
    lane = lax.broadcasted_iota(jnp.int32, (qb, HEAD_WIDTH), 1)
    zero = jnp.zeros((qb, HEAD_WIDTH), BF16)

    def rows(i):
        return pl.ds(pl.multiple_of(i * qb, qb), qb)

    def stage(i_values, i_exps, i_scores):
        if i_values is not None:
            l1 = jnp.sum(l_ref[0], axis=-1, keepdims=True)
            l2 = jnp.sum(l_ref[1], axis=-1, keepdims=True)
            ratio = (lam * l1 / l2).astype(BF16)
            o = jnp.zeros((qb, HEAD_WIDTH), F32)
        if i_exps is not None:
            rowmax = [jnp.max(m_ref[c], axis=-1, keepdims=True) for c in range(2)]
            lpart = [jnp.zeros((qb, LANES), F32) for _ in range(2)]
        if i_scores is not None:
            q = q_ref[0, rows(i_scores), :]
            qc = (jnp.where(lane < HEAD_DIM, q, zero), jnp.where(lane < HEAD_DIM, zero, q))
            mpart = [jnp.full((qb, LANES), -jnp.inf, F32) for _ in range(2)]
        for j in range(nk):
            keys = slice(j * kc, (j + 1) * kc)
            tie = None
            if i_exps is not None and j > 0:
                both = jnp.max(jnp.maximum(lpart[0], lpart[1]), axis=0, keepdims=True)
                tie = jnp.where(both > 1e30, 1.0, 0.0).astype(BF16)
            if i_values is not None:
                w = p_ref[0, j] - p_ref[1, j] * ratio
                if tie is not None:
                    w = jnp.concatenate([w[:, t * LANES:(t + 1) * LANES] + tie for t in range(kc // LANES)], axis=1)
                o = o + jnp.dot(w, v_ref[0, keys, :], preferred_element_type=F32)
            if i_exps is not None:
                for c in range(2):
                    e = jnp.exp2(s_ref[c, j] - rowmax[c])
                    p_ref[c, j] = e.astype(BF16)
                    lpart[c] = lpart[c] + _lane_fold(e, jnp.add)
            if i_scores is not None:
                for c in range(2):
                    qcj = qc[c] if tie is None else qc[c] + tie
                    s = lax.dot_general(qcj, k_ref[0, keys, :], (((1,), (1,)), ((), ())),
                                        preferred_element_type=F32)
                    s_ref[c, j] = s
                    mpart[c] = jnp.maximum(mpart[c], _lane_fold(s, jnp.maximum))
        for c in range(2):
            if i_exps is not None:
                l_ref[c] = lpart[c]
            if i_scores is not None:
                m_ref[c] = mpart[c]
        if i_values is not None:
            o = o * (1.0 / l1)
            o = _rms(o, SUBLN_EPS) * sg_ref[...] * (1.0 - lam_init)
            o_ref[0, rows(i_values), :] = o.astype(BF16)

    def steady(i, carry):
        stage(i - 1, i, i + 1)
        return carry

    stage(None, None, 0)
    stage(None, 0, 1)
    lax.fori_loop(1, nb - 1, steady, 0)
    stage(nb - 2, nb - 1, None)
    stage(nb - 1, None, None)


def _attention(q, k, v, lq1, lk1, lq2, lk2, subln_gain, lam_init, qb, kc):
    b, s, aw = q.shape
    heads = aw // HEAD_WIDTH
    assert s // qb >= 2
    vec = lambda a: pl.BlockSpec(a.shape, lambda bi, h: (0, 0))
    seqblk = pl.BlockSpec((1, s, HEAD_WIDTH), lambda bi, h: (bi, 0, h))
    return pl.pallas_call(
        functools.partial(_attn_kernel, lam_init=lam_init, qb=qb, kc=kc),
        grid=(b, heads),
        in_specs=[vec(lq1), vec(lk1), vec(lq2), vec(lk2), vec(subln_gain), seqblk, seqblk, seqblk],
        out_specs=seqblk,
        out_shape=jax.ShapeDtypeStruct((b, s, aw), BF16),
        scratch_shapes=[pltpu.VMEM((2, s // kc, qb, kc), F32), pltpu.VMEM((2, s // kc, qb, kc), BF16),
                        pltpu.VMEM((2, qb, LANES), F32), pltpu.VMEM((2, qb, LANES), F32)],
        compiler_params=_cparams(("parallel", "parallel"), 48),
        name="diffattn",
    )(lq1, lk1, lq2, lk2, subln_gain, q, k, v)


def _outproj_kernel(yh_ref, ya_ref, w_ref, g_ref, x_ref, o_ref, *, c):
    mix = (jnp.dot(yh_ref[...], w_ref[:c, :], preferred_element_type=F32)
           + jnp.dot(ya_ref[...], w_ref[c:, :], preferred_element_type=F32))
    o_ref[...] = x_ref[...] + _rms(mix, NORM_EPS) * g_ref[...]


def _outproj(yh, ya, w_bf, gain, x2, tm):
    t, d = x2.shape
    c = yh.shape[1]
    row = lambda n: pl.BlockSpec((tm, n), lambda i: (i, 0))
    return pl.pallas_call(
        functools.partial(_outproj_kernel, c=c),
        grid=(t // tm,),
        in_specs=[row(c), row(ya.shape[1]), pl.BlockSpec(w_bf.shape, lambda i: (0, 0)),
                  pl.BlockSpec((1, d), lambda i: (0, 0)), row(d)],
        out_specs=row(d),
        out_shape=jax.ShapeDtypeStruct((t, d), F32),
        compiler_params=_cparams(("parallel",), 48),
        name="outproj",
    )(yh, ya, w_bf, gain, x2)


def _mlp_kernel(x_ref, gpre_ref, wu_ref, wd_ref, gpost_ref, o_ref, *, fc):
    x = x_ref[...]
    hb = (_rms(x, NORM_EPS) * gpre_ref[...]).astype(BF16)
    acc = jnp.zeros(x.shape, F32)
    for j in range(wu_ref.shape[1] // fc):
        up = jnp.dot(hb, wu_ref[:, j * fc:(j + 1) * fc], preferred_element_type=F32)
        act = jnp.square(jnp.maximum(up, 0.0)).astype(BF16)
        acc = acc + jnp.dot(act, wd_ref[j * fc:(j + 1) * fc, :], preferred_element_type=F32)
    o_ref[...] = x + _rms(acc, NORM_EPS) * gpost_ref[...]


def _mlp(x2, gpre, wu_bf, wd_bf, gpost, tm, fc):
    t, d = x2.shape
    row = pl.BlockSpec((tm, d), lambda i: (i, 0))
    vec = pl.BlockSpec((1, d), lambda i: (0, 0))
    resident = lambda a: pl.BlockSpec(a.shape, lambda i: (0, 0), pipeline_mode=pl.Buffered(1))
    return pl.pallas_call(
        functools.partial(_mlp_kernel, fc=fc),
        grid=(t // tm,),
        in_specs=[row, vec, resident(wu_bf), resident(wd_bf), vec],
        out_specs=row,
        out_shape=jax.ShapeDtypeStruct((t, d), F32),
        compiler_params=_cparams(("parallel",), 56),
        name="mlp",
    )(x2, gpre, wu_bf, wd_bf, gpost)


def _tile(n, want):
    return min(n, want)


def kernel(x, attn_pre_gain, attn_post_gain, w_in, conv_w, conv_b, filt_w1, filt_b1, filt_w2, filt_b2, filt_w3, filt_b3, filt_w4, filt_freq, filt_bias, lam_q1, lam_k1, lam_q2, lam_k2, subln_gain, w_out, mlp_pre_gain, mlp_post_gain, w_up, w_down):
    b, s, d = x.shape
    depth = w_in.shape[0]
    c = filt_bias.shape[1]
    hy3 = 3 * c
    aw = (w_in.shape[2] - hy3) // 3
    t = b * s
    tm = _tile(t, 512)
    row = lambda a, l: a[l][None, :]
    x2 = x.reshape(t, d)
    for l in range(depth):
        u, q, k, v = _inproj(x2, row(attn_pre_gain, l), w_in[l].astype(BF16), s, hy3, aw, _tile(s, 512))
        g, x0 = _hygate(u.reshape(b, s, hy3), conv_w[l], row(conv_b, l), _tile(s, 512))
        hfb = _filters(s, c, filt_w1[l], row(filt_b1, l), filt_w2[l], row(filt_b2, l), filt_w3[l],
                       row(filt_b3, l), filt_w4[l], row(filt_freq, l), _tile(s, 512))
        y_hy = _long_conv_gated(g, x0, hfb, row(filt_bias, l))
        lam_init = 0.8 - 0.6 * math.exp(-0.3 * l)
        y_at = _attention(q.reshape(b, s, aw), k.reshape(b, s, aw), v.reshape(b, s, aw),
                          row(lam_q1, l), row(lam_k1, l), row(lam_q2, l), row(lam_k2, l),
                          row(subln_gain, l), lam_init, _tile(s, 256), _tile(s, 512))
        x2 = _outproj(y_hy.reshape(t, c), y_at.reshape(t, aw), w_out[l].astype(BF16),
                      row(attn_post_gain, l), x2, tm)
        x2 = _mlp(x2, row(mlp_pre_gain, l), w_up[l].astype(BF16), w_down[l].astype(BF16),
                  row(mlp_post_gain, l), tm, _tile(w_up.shape[2], 1024))
    return x2.reshape(b, s, d)
```

```python
import functools
import math

import jax
import jax.numpy as jnp
from jax import lax
from jax.experimental import pallas as pl
from jax.experimental.pallas import tpu as pltpu

F32 = jnp.float32
BF16 = jnp.bfloat16

HEAD_DIM = 64
HEAD_WIDTH = 2 * HEAD_DIM
FILTER_EMB_DIM = 33
FILTER_TARGET = 1e-2
FILTER_FAST_DECAY_PCT = 0.3
FILTER_SLOW_DECAY_PCT = 1.5
ROPE_THETA = 10000.0
NORM_EPS = 1e-6
SUBLN_EPS = 1e-5

LANES = 128
BF16_SUBLANES = 16
MIB = 1024 * 1024


def _cparams(semantics, vmem_mib):
    return pltpu.CompilerParams(dimension_semantics=semantics, vmem_limit_bytes=vmem_mib * MIB)


def _rms(x, eps):
    return x * lax.rsqrt(jnp.mean(x * x, axis=-1, keepdims=True) + eps)


def _inproj_kernel(x_ref, g_ref, w_ref, cos_ref, sa_ref, sb_ref, u_ref, q_ref, k_ref, v_ref, *, hy3, aw, scale):
    x = x_ref[...]
    hb = (_rms(x, NORM_EPS) * g_ref[...]).astype(BF16)
    u_ref[...] = jnp.dot(hb, w_ref[:, :hy3], preferred_element_type=F32).astype(BF16)
    cos, sa, sb = cos_ref[...], sa_ref[...], sb_ref[...]

    def rope(t):
        return t * cos + pltpu.roll(t, LANES - HEAD_DIM // 2, 1) * sa + pltpu.roll(t, HEAD_DIM // 2, 1) * sb

    q = jnp.dot(hb, w_ref[:, hy3:hy3 + aw], preferred_element_type=F32)
    k = jnp.dot(hb, w_ref[:, hy3 + aw:hy3 + 2 * aw], preferred_element_type=F32)
    for j in range(aw // LANES):
        sl = slice(j * LANES, (j + 1) * LANES)
        q_ref[:, sl] = (rope(q[:, sl]) * scale).astype(BF16)
        k_ref[:, sl] = rope(k[:, sl]).astype(BF16)
    v_ref[...] = jnp.dot(hb, w_ref[:, hy3 + 2 * aw:], preferred_element_type=F32).astype(BF16)


def _rope_tables(seq):
    inv = ROPE_THETA ** (-jnp.arange(0, HEAD_DIM, 2, dtype=F32) / HEAD_DIM)
    ang = jnp.arange(seq, dtype=F32)[:, None] * inv[None, :]
    ang = jnp.tile(ang, (1, LANES // (HEAD_DIM // 2)))
    first = (jnp.arange(LANES) % HEAD_DIM) < HEAD_DIM // 2
    sin = jnp.sin(ang)
    return jnp.cos(ang), jnp.where(first, -sin, 0.0), jnp.where(first, 0.0, sin)


def _inproj(x2, gain, w_bf, seq, hy3, aw, tm):
    t, d = x2.shape
    cos, sa, sb = _rope_tables(seq)
    nseq = seq // tm
    tab = pl.BlockSpec((tm, LANES), lambda i: (i % nseq, 0))
    row = lambda n: pl.BlockSpec((tm, n), lambda i: (i, 0))
    return pl.pallas_call(
        functools.partial(_inproj_kernel, hy3=hy3, aw=aw, scale=HEAD_DIM ** -0.5 * math.log2(math.e)),
        grid=(t // tm,),
        in_specs=[row(d), pl.BlockSpec((1, d), lambda i: (0, 0)),
                  pl.BlockSpec(w_bf.shape, lambda i: (0, 0)), tab, tab, tab],
        out_specs=[row(hy3), row(aw), row(aw), row(aw)],
        out_shape=[jax.ShapeDtypeStruct((t, hy3), BF16)] + [jax.ShapeDtypeStruct((t, aw), BF16)] * 3,
        compiler_params=_cparams(("parallel",), 48),
        name="inproj",
    )(x2, gain, w_bf, cos, sa, sb)


def _hygate_kernel(prev_ref, cur_ref, next_ref, w_ref, b_ref, g_ref, x0_ref, *, c):
    i = pl.program_id(1)
    last = pl.num_programs(1) - 1
    u = cur_ref[0].astype(F32)
    tl = u.shape[0]
    prev_row = prev_ref[0, BF16_SUBLANES - 1:BF16_SUBLANES, :].astype(F32) * jnp.where(i > 0, 1.0, 0.0)
    next_row = next_ref[0, 0:1, :].astype(F32) * jnp.where(i < last, 1.0, 0.0)
    rows = lax.broadcasted_iota(jnp.int32, u.shape, 0)
    up = jnp.where(rows == 0, prev_row, pltpu.roll(u, 1, 0))
    dn = jnp.where(rows == tl - 1, next_row, pltpu.roll(u, tl - 1, 0))
    y = up * w_ref[0:1, :] + u * w_ref[1:2, :] + dn * w_ref[2:3, :] + b_ref[...]
    x0_ref[0] = y[:, :c].astype(BF16)
    g = y[:, 2 * c:] * y[:, c:2 * c]
    for t in range(c // LANES):
        g_ref[0, t] = g[:, t * LANES:(t + 1) * LANES]


def _hygate(u, conv_w, conv_b, tl):
    b, s, c3 = u.shape
    c = c3 // 3
    nct = c // LANES
    nb = tl // BF16_SUBLANES
    nhalo = s // BF16_SUBLANES
    return pl.pallas_call(
        functools.partial(_hygate_kernel, c=c),
        grid=(b, s // tl),
        in_specs=[
            pl.BlockSpec((1, BF16_SUBLANES, c3), lambda bi, i: (bi, jnp.maximum(i * nb - 1, 0), 0)),
            pl.BlockSpec((1, tl, c3), lambda bi, i: (bi, i, 0)),
            pl.BlockSpec((1, BF16_SUBLANES, c3), lambda bi, i: (bi, jnp.minimum((i + 1) * nb, nhalo - 1), 0)),
            pl.BlockSpec((3, c3), lambda bi, i: (0, 0)),
            pl.BlockSpec((1, c3), lambda bi, i: (0, 0)),
        ],
        out_specs=[pl.BlockSpec((1, nct, tl, LANES), lambda bi, i: (bi, 0, i, 0)),
                   pl.BlockSpec((1, tl, c), lambda bi, i: (bi, i, 0))],
        out_shape=[jax.ShapeDtypeStruct((b, nct, s, LANES), F32), jax.ShapeDtypeStruct((b, s, c), BF16)],
        compiler_params=_cparams(("parallel", "parallel"), 48),
        name="hygate",
    )(u, u, u, conv_w, conv_b)


def _filter_kernel(z_ref, dec_ref, w1_ref, b1_ref, w2_ref, b2_ref, w3_ref, b3_ref, w4_ref, fr_ref, o_ref, *, c):
    dot = functools.partial(jnp.dot, preferred_element_type=F32, precision=lax.Precision.HIGHEST)
    fr = fr_ref[...]
    h = jnp.sin(fr * (dot(z_ref[...], w1_ref[...]) + b1_ref[...]))
    h = jnp.sin(fr * (dot(h, w2_ref[...]) + b2_ref[...]))
    h = jnp.sin(fr * (dot(h, w3_ref[...]) + b3_ref[...]))
    h = dot(h, w4_ref[...])
    dec = dec_ref[...]
    for d in range(2):
        hd = h[:, d * c:(d + 1) * c] * dec
        for t in range(c // LANES):
            o_ref[d, t] = hd[:, t * LANES:(t + 1) * LANES]


def _filter_inputs(seq, c):
    bands = (FILTER_EMB_DIM - 1) // 2
    t = jnp.linspace(0.0, 1.0, seq, dtype=F32)[:, None]
    w = (2.0 * math.pi / seq) * jnp.arange(seq, dtype=F32)[:, None]
    f = jnp.linspace(1e-4, bands - 1, bands, dtype=F32)[None, :]
    fw = f * w
    z = jnp.concatenate([t, jnp.cos(fw), -jnp.sin(fw)], axis=-1)
    max_decay = math.log(FILTER_TARGET) / FILTER_FAST_DECAY_PCT
    min_decay = math.log(FILTER_TARGET) / FILTER_SLOW_DECAY_PCT
    deltas = jnp.linspace(min_decay, max_decay, c, dtype=F32)[None, :]
    return z, jnp.exp(-t * jnp.abs(deltas))


def _filters(seq, c, w1, b1, w2, b2, w3, b3, w4, freq, ts):
    z, decay = _filter_inputs(seq, c)
    order = w1.shape[1]
    zp = jnp.pad(z, ((0, 0), (0, order - FILTER_EMB_DIM)))
    w1p = jnp.pad(w1, ((0, order - FILTER_EMB_DIM), (0, 0)))
    full = lambda a: pl.BlockSpec(a.shape, lambda i: (0,) * a.ndim)
    args = (zp, decay, w1p, b1, w2, b2, w3, b3, w4, freq)
    return pl.pallas_call(
        functools.partial(_filter_kernel, c=c),
        grid=(seq // ts,),
        in_specs=[pl.BlockSpec((ts, order), lambda i: (i, 0)), pl.BlockSpec((ts, c), lambda i: (i, 0))]
        + [full(a) for a in args[2:]],
        out_specs=pl.BlockSpec((2, c // LANES, ts, LANES), lambda i: (0, 0, i, 0)),
        out_shape=jax.ShapeDtypeStruct((2, c // LANES, seq, LANES), F32),
        compiler_params=_cparams(("parallel",), 48),
        name="hyfilter",
    )(*args)


def _cis(m, n):
    ang = (m % n).astype(F32) * (2.0 * math.pi / n)
    return jnp.cos(ang), jnp.sin(ang)


def _dft_tables(n1, n2):
    n = n1 * n2
    n1h = n1 // 2
    k1 = jnp.arange(n1, dtype=jnp.int32)[:, None]
    j1 = jnp.arange(n1h, dtype=jnp.int32)[None, :]
    c, s = _cis(k1 * j1, n1)
    fa = jnp.stack([c, -s], axis=1).reshape(2 * n1, n1h)
    fb = jnp.stack([s, c], axis=1).reshape(2 * n1, n1h)
    f1 = jnp.concatenate([fa, fb], axis=1)
    kk = (jnp.arange(n1, dtype=jnp.int32)[:, None, None] + n1 * jnp.arange(n2, dtype=jnp.int32)[None, :, None])
    j2 = jnp.arange(n2, dtype=jnp.int32)[None, None, :]
    c, s = _cis(kk * j2, n)
    m2 = jnp.concatenate([jnp.concatenate([c, s], axis=2), jnp.concatenate([-s, c], axis=2)], axis=1)
    ct, st = jnp.swapaxes(c, 1, 2), jnp.swapaxes(s, 1, 2)
    m2i = jnp.concatenate([jnp.concatenate([ct, -st], axis=2), jnp.concatenate([st, ct], axis=2)], axis=1)
    j1 = jnp.arange(n1h, dtype=jnp.int32)[:, None]
    k1 = jnp.arange(n1, dtype=jnp.int32)[None, :]
    c, s = _cis(j1 * k1, n1)
    c, s = c / n, s / n
    g_re = jnp.stack([c, -s], axis=2).reshape(n1h, 2 * n1)
    g_im = jnp.stack([s, c], axis=2).reshape(n1h, 2 * n1)
    g1 = jnp.concatenate([g_re, g_im], axis=0)
    return f1.astype(BF16), m2.astype(BF16), m2i.astype(BF16), g1.astype(BF16)


def _fft_split(seq):
    n1 = min(128, seq // 8)
    return n1, 2 * seq // n1


FFT_UNROLL = 8


def _hyspec_kernel(fa_ref, m2_ref, bias_ref, hf_ref, hb_ref, h_ref, af_ref, ab_ref, *, n1, n2):
    n1h = n1 // 2
    fa = fa_ref[...]

    def stage1(j2, carry):
        src = pl.ds(j2, n1h, stride=n2)
        dst = pl.ds(j2, 2 * n1, stride=n2)
        af_ref[dst, :] = jnp.dot(fa, hf_ref[0, 0, src, :].astype(BF16), preferred_element_type=F32)
        ab_ref[dst, :] = jnp.dot(fa, hb_ref[0, 0, src, :].astype(BF16), preferred_element_type=F32)
        return carry

    lax.fori_loop(0, n2, stage1, 0, unroll=min(FFT_UNROLL, n2))
    bias = bias_ref[...]

    def stage2(k1, carry):
        rows = pl.ds(pl.multiple_of(k1 * 2 * n2, 2 * n2), 2 * n2)
        xf = jnp.dot(m2_ref[k1], af_ref[rows, :].astype(BF16), preferred_element_type=F32)
        xb = jnp.dot(m2_ref[k1], ab_ref[rows, :].astype(BF16), preferred_element_type=F32)
        h_ref[k1, :n2, :] = (xf[:n2] + xb[:n2] + bias).astype(BF16)
        h_ref[k1, n2:, :] = (xf[n2:] - xb[n2:]).astype(BF16)
        return carry

    lax.fori_loop(0, n1, stage2, 0, unroll=min(FFT_UNROLL, n1))


def _hyconv_kernel(f1_ref, m2_ref, m2i_ref, g1_ref, h_ref, ga_ref, gb_ref, o_ref, a_ref, *, n1, n2):
    n1h = n1 // 2
    f1 = f1_ref[...]

    def stage1(j2, carry):
        src = pl.ds(j2, n1h, stride=n2)
        z = jnp.concatenate([ga_ref[0, 0, src, :], gb_ref[0, 0, src, :]], axis=0).astype(BF16)
        a_ref[pl.ds(j2, 2 * n1, stride=n2), :] = jnp.dot(f1, z, preferred_element_type=F32)
        return carry

    lax.fori_loop(0, n2, stage1, 0, unroll=min(FFT_UNROLL, n2))

    def stage2(k1, carry):
        rows = pl.ds(pl.multiple_of(k1 * 2 * n2, 2 * n2), 2 * n2)
        x = jnp.dot(m2_ref[k1], a_ref[rows, :].astype(BF16), preferred_element_type=F32)
        xr, xi = x[:n2], x[n2:]
        hr, hi = h_ref[k1, :n2, :].astype(F32), h_ref[k1, n2:, :].astype(F32)
        y = jnp.concatenate([xr * hr - xi * hi, xr * hi + xi * hr], axis=0).astype(BF16)
        a_ref[rows, :] = jnp.dot(m2i_ref[k1], y, preferred_element_type=F32)
        return carry

    lax.fori_loop(0, n1, stage2, 0, unroll=min(FFT_UNROLL, n1))
    g1 = g1_ref[...]

    def stage3(j2, carry):
        bs = a_ref[pl.ds(j2, 2 * n1, stride=n2), :].astype(BF16)
        y = jnp.dot(g1, bs, preferred_element_type=F32)
        dst = pl.ds(j2, n1h, stride=n2)
        o_ref[0, 0, 0, dst, :] = y[:n1h]
        o_ref[1, 0, 0, dst, :] = y[n1h:]
        return carry

    lax.fori_loop(0, n2, stage3, 0, unroll=min(FFT_UNROLL, n2))


def _long_conv(g, hfb, filt_bias):
    b, nct, seq, _ = g.shape
    pairs = b // 2
    n1, n2 = _fft_split(seq)
    f1, m2, m2i, g1 = _dft_tables(n1, n2)
    fa = f1[:, :n1 // 2]
    rows = 2 * n1 * n2
    const = lambda a, nd: pl.BlockSpec(a.shape, (lambda t: (0,) * a.ndim) if nd == 1 else (lambda t, p: (0,) * a.ndim),
                                       pipeline_mode=pl.Buffered(1))
    h = pl.pallas_call(
        functools.partial(_hyspec_kernel, n1=n1, n2=n2),
        grid=(nct,),
        in_specs=[const(fa, 1), const(m2, 1), pl.BlockSpec((1, LANES), lambda t: (0, t)),
                  pl.BlockSpec((1, 1, seq, LANES), lambda t: (0, t, 0, 0)),
                  pl.BlockSpec((1, 1, seq, LANES), lambda t: (1, t, 0, 0))],
        out_specs=pl.BlockSpec((n1, 2 * n2, LANES), lambda t: (0, 0, t)),
        out_shape=jax.ShapeDtypeStruct((n1, 2 * n2, nct * LANES), BF16),
        scratch_shapes=[pltpu.VMEM((rows, LANES), F32), pltpu.VMEM((rows, LANES), F32)],
        compiler_params=_cparams(("parallel",), 48),
        name="hyspectrum",
    )(fa, m2, filt_bias, hfb, hfb)
    seqblk = lambda off: pl.BlockSpec((1, 1, seq, LANES), lambda t, p: (p + off, t, 0, 0))
    y = pl.pallas_call(
        functools.partial(_hyconv_kernel, n1=n1, n2=n2),
        grid=(nct, pairs),
        in_specs=[const(f1, 2), const(m2, 2), const(m2i, 2), const(g1, 2),
                  pl.BlockSpec((n1, 2 * n2, LANES), lambda t, p: (0, 0, t)), seqblk(0), seqblk(pairs)],
        out_specs=pl.BlockSpec((2, 1, 1, seq, LANES), lambda t, p: (0, p, t, 0, 0)),
        out_shape=jax.ShapeDtypeStruct((2, pairs, nct, seq, LANES), F32),
        scratch_shapes=[pltpu.VMEM((rows, LANES), F32)],
        compiler_params=_cparams(("parallel", "parallel"), 48),
        name="hyconv",
    )(f1, m2, m2i, g1, h, g, g)
    return y.reshape(b, nct, seq, LANES)


def _lane_fold(x, op):
    acc = x[:, :LANES]
    for t in range(1, x.shape[1] // LANES):
        acc = op(acc, x[:, t * LANES:(t + 1) * LANES])
    return acc


def _attn_kernel(lq1_ref, lk1_ref, lq2_ref, lk2_ref, sg_ref, q_ref, k_ref, v_ref, o_ref,
                 s_ref, p_ref, m_ref, l_ref, *, lam_init, qb, kc):
    seq = k_ref.shape[1]
    nk, nb = seq // kc, seq // qb
    lam = (jnp.exp(jnp.sum(lq1_ref[...] * lk1_ref[...], axis=-1, keepdims=True))
           - jnp.exp(jnp.sum(lq2_ref[...] * lk2_ref[...], axis=-1, keepdims=True)) + lam_init)
    lane = lax.broadcasted_iota(jnp.int32, (qb, HEAD_WIDTH), 1)
    zero = jnp.zeros((qb, HEAD_WIDTH), BF16)

    def rows(i):
        return pl.ds(pl.multiple_of(i * qb, qb), qb)

    def stage(i_values, i_exps, i_scores):
        if i_values is not None:
            l1 = jnp.sum(l_ref[0], axis=-1, keepdims=True)
            l2 = jnp.sum(l_ref[1], axis=-1, keepdims=True)
            ratio = (lam * l1 / l2).astype(BF16)
            o = jnp.zeros((qb, HEAD_WIDTH), F32)
        if i_exps is not None:
            rowmax = [jnp.max(m_ref[c], axis=-1, keepdims=True) for c in range(2)]
            lpart = [jnp.zeros((qb, LANES), F32) for _ in range(2)]
        if i_scores is not None:
            q = q_ref[0, rows(i_scores), :]
            qc = (jnp.where(lane < HEAD_DIM, q, zero), jnp.where(lane < HEAD_DIM, zero, q))
            mpart = [jnp.full((qb, LANES), -jnp.inf, F32) for _ in range(2)]
        for j in range(nk):
            keys = slice(j * kc, (j + 1) * kc)
            tie = None
            if i_exps is not None and j > 0:
                both = jnp.max(jnp.maximum(lpart[0], lpart[1]), axis=0, keepdims=True)
                tie = jnp.where(both > 1e30, 1.0, 0.0).astype(BF16)
            if i_values is not None:
                w = p_ref[0, j] - p_ref[1, j] * ratio
                if tie is not None:
                    w = jnp.concatenate([w[:, t * LANES:(t + 1) * LANES] + tie for t in range(kc // LANES)], axis=1)
                o = o + jnp.dot(w, v_ref[0, keys, :], preferred_element_type=F32)
            if i_exps is not None:
                for c in range(2):
                    e = jnp.exp2(s_ref[c, j] - rowmax[c])
                    p_ref[c, j] = e.astype(BF16)
                    lpart[c] = lpart[c] + _lane_fold(e, jnp.add)
            if i_scores is not None:
                for c in range(2):
                    qcj = qc[c] if tie is None else qc[c] + tie
                    s = lax.dot_general(qcj, k_ref[0, keys, :], (((1,), (1,)), ((), ())),
                                        preferred_element_type=F32)
                    s_ref[c, j] = s
                    mpart[c] = jnp.maximum(mpart[c], _lane_fold(s, jnp.maximum))
        for c in range(2):
            if i_exps is not None:
                l_ref[c] = lpart[c]
            if i_scores is not None:
                m_ref[c] = mpart[c]
        if i_values is not None:
            o = o * (1.0 / l1)
            o = _rms(o, SUBLN_EPS) * sg_ref[...] * (1.0 - lam_init)
            o_ref[0, rows(i_values), :] = o.astype(BF16)

    def steady(i, carry):
        stage(i - 1, i, i + 1)
        return carry

    stage(None, None, 0)
    stage(None, 0, 1)
    lax.fori_loop(1, nb - 1, steady, 0)
    stage(nb - 2, nb - 1, None)
    stage(nb - 1, None, None)


def _attention(q, k, v, lq1, lk1, lq2, lk2, subln_gain, lam_init, qb, kc):
    b, s, aw = q.shape
    heads = aw // HEAD_WIDTH
    assert s // qb >= 2
    vec = lambda a: pl.BlockSpec(a.shape, lambda bi, h: (0, 0))
    seqblk = pl.BlockSpec((1, s, HEAD_WIDTH), lambda bi, h: (bi, 0, h))
    return pl.pallas_call(
        functools.partial(_attn_kernel, lam_init=lam_init, qb=qb, kc=kc),
        grid=(b, heads),
        in_specs=[vec(lq1), vec(lk1), vec(lq2), vec(lk2), vec(subln_gain), seqblk, seqblk, seqblk],
        out_specs=seqblk,
        out_shape=jax.ShapeDtypeStruct((b, s, aw), BF16),
        scratch_shapes=[pltpu.VMEM((2, s // kc, qb, kc), F32), pltpu.VMEM((2, s // kc, qb, kc), BF16),
                        pltpu.VMEM((2, qb, LANES), F32), pltpu.VMEM((2, qb, LANES), F32)],
        compiler_params=_cparams(("parallel", "parallel"), 48),
        name="diffattn",
    )(lq1, lk1, lq2, lk2, subln_gain, q, k, v)


def _outproj_kernel(yc_ref, x0_ref, ya_ref, w_ref, g_ref, x_ref, o_ref, *, c):
    conv = jnp.concatenate([yc_ref[0, t] for t in range(c // LANES)], axis=1)
    yh = (conv * x0_ref[0].astype(F32)).astype(BF16)
    mix = (jnp.dot(yh, w_ref[:c, :], preferred_element_type=F32)
           + jnp.dot(ya_ref[0], w_ref[c:, :], preferred_element_type=F32))
    o_ref[0] = x_ref[0] + _rms(mix, NORM_EPS) * g_ref[...]


def _outproj(yc, x0, ya, w_bf, gain, x, tm):
    b, s, d = x.shape
    c = x0.shape[2]
    row = lambda n: pl.BlockSpec((1, tm, n), lambda bi, i: (bi, i, 0))
    return pl.pallas_call(
        functools.partial(_outproj_kernel, c=c),
        grid=(b, s // tm),
        in_specs=[pl.BlockSpec((1, c // LANES, tm, LANES), lambda bi, i: (bi, 0, i, 0)), row(c), row(ya.shape[2]),
                  pl.BlockSpec(w_bf.shape, lambda bi, i: (0, 0)), pl.BlockSpec((1, d), lambda bi, i: (0, 0)), row(d)],
        out_specs=row(d),
        out_shape=jax.ShapeDtypeStruct((b, s, d), F32),
        compiler_params=_cparams(("parallel", "parallel"), 48),
        name="outproj",
    )(yc, x0, ya, w_bf, gain, x)


def _mlp_kernel(x_ref, gpre_ref, wu_ref, wd_ref, gpost_ref, o_ref, *, fc):
    x = x_ref[...]
    hb = (_rms(x, NORM_EPS) * gpre_ref[...]).astype(BF16)
    acc = jnp.zeros(x.shape, F32)
    for j in range(wu_ref.shape[1] // fc):
        up = jnp.dot(hb, wu_ref[:, j * fc:(j + 1) * fc], preferred_element_type=F32)
        act = jnp.square(jnp.maximum(up, 0.0)).astype(BF16)
        acc = acc + jnp.dot(act, wd_ref[j * fc:(j + 1) * fc, :], preferred_element_type=F32)
    o_ref[...] = x + _rms(acc, NORM_EPS) * gpost_ref[...]


def _mlp(x2, gpre, wu_bf, wd_bf, gpost, tm, fc):
    t, d = x2.shape
    row = pl.BlockSpec((tm, d), lambda i: (i, 0))
    vec = pl.BlockSpec((1, d), lambda i: (0, 0))
    resident = lambda a: pl.BlockSpec(a.shape, lambda i: (0, 0), pipeline_mode=pl.Buffered(1))
    return pl.pallas_call(
        functools.partial(_mlp_kernel, fc=fc),
        grid=(t // tm,),
        in_specs=[row, vec, resident(wu_bf), resident(wd_bf), vec],
        out_specs=row,
        out_shape=jax.ShapeDtypeStruct((t, d), F32),
        compiler_params=_cparams(("parallel",), 56),
        name="mlp",
    )(x2, gpre, wu_bf, wd_bf, gpost)


def _tile(n, want):
    return min(n, want)


def kernel(x, attn_pre_gain, attn_post_gain, w_in, conv_w, conv_b, filt_w1, filt_b1, filt_w2, filt_b2, filt_w3, filt_b3, filt_w4, filt_freq, filt_bias, lam_q1, lam_k1, lam_q2, lam_k2, subln_gain, w_out, mlp_pre_gain, mlp_post_gain, w_up, w_down):
    b, s, d = x.shape
    depth = w_in.shape[0]
    c = filt_bias.shape[1]
    hy3 = 3 * c
    aw = (w_in.shape[2] - hy3) // 3
    t = b * s
    tm = _tile(t, 512)
    row = lambda a, l: a[l][None, :]
    x2 = x.reshape(t, d)
    for l in range(depth):
        u, q, k, v = _inproj(x2, row(attn_pre_gain, l), w_in[l].astype(BF16), s, hy3, aw, _tile(s, 512))
        g, x0 = _hygate(u.reshape(b, s, hy3), conv_w[l], row(conv_b, l), _tile(s, 512))
        hfb = _filters(s, c, filt_w1[l], row(filt_b1, l), filt_w2[l], row(filt_b2, l), filt_w3[l],
                       row(filt_b3, l), filt_w4[l], row(filt_freq, l), _tile(s, 512))
        yc = _long_conv(g, hfb, row(filt_bias, l))
        lam_init = 0.8 - 0.6 * math.exp(-0.3 * l)
        y_at = _attention(q.reshape(b, s, aw), k.reshape(b, s, aw), v.reshape(b, s, aw),
                          row(lam_q1, l), row(lam_k1, l), row(lam_q2, l), row(lam_k2, l),
                          row(subln_gain, l), lam_init, _tile(s, 256), _tile(s, 512))
        x2 = _outproj(yc, x0, y_at, w_out[l].astype(BF16), row(attn_post_gain, l),
                      x2.reshape(b, s, d), _tile(s, 512)).reshape(t, d)
        x2 = _mlp(x2, row(mlp_pre_gain, l), w_up[l].astype(BF16), w_down[l].astype(BF16),
                  row(mlp_post_gain, l), tm, _tile(w_up.shape[2], 1024))
    return x2.reshape(b, s, d)
```

```python
import functools
import math

import jax
import jax.numpy as jnp
from jax import lax
from jax.experimental import pallas as pl
from jax.experimental.pallas import tpu as pltpu

F32 = jnp.float32
BF16 = jnp.bfloat16

HEAD_DIM = 64
HEAD_WIDTH = 2 * HEAD_DIM
FILTER_EMB_DIM = 33
FILTER_TARGET = 1e-2
FILTER_FAST_DECAY_PCT = 0.3
FILTER_SLOW_DECAY_PCT = 1.5
ROPE_THETA = 10000.0
NORM_EPS = 1e-6
SUBLN_EPS = 1e-5

LANES = 128
BF16_SUBLANES = 16
MIB = 1024 * 1024


def _cparams(semantics, vmem_mib):
    return pltpu.CompilerParams(dimension_semantics=semantics, vmem_limit_bytes=vmem_mib * MIB)


def _rms(x, eps):
    return x * lax.rsqrt(jnp.mean(x * x, axis=-1, keepdims=True) + eps)


def _inproj_kernel(x_ref, g_ref, w_ref, cos_ref, sa_ref, sb_ref, u_ref, q_ref, k_ref, v_ref, *, hy3, aw, scale):
    x = x_ref[...]
    hb = (_rms(x, NORM_EPS) * g_ref[...]).astype(BF16)
    u_ref[...] = jnp.dot(hb, w_ref[:, :hy3], preferred_element_type=F32).astype(BF16)
    cos, sa, sb = cos_ref[...], sa_ref[...], sb_ref[...]

    def rope(t):
        return t * cos + pltpu.roll(t, LANES - HEAD_DIM // 2, 1) * sa + pltpu.roll(t, HEAD_DIM // 2, 1) * sb

    q = jnp.dot(hb, w_ref[:, hy3:hy3 + aw], preferred_element_type=F32)
    k = jnp.dot(hb, w_ref[:, hy3 + aw:hy3 + 2 * aw], preferred_element_type=F32)
    for j in range(aw // LANES):
        sl = slice(j * LANES, (j + 1) * LANES)
        q_ref[:, sl] = (rope(q[:, sl]) * scale).astype(BF16)
        k_ref[:, sl] = rope(k[:, sl]).astype(BF16)
    v_ref[...] = jnp.dot(hb, w_ref[:, hy3 + 2 * aw:], preferred_element_type=F32).astype(BF16)


def _rope_tables(seq):
    inv = ROPE_THETA ** (-jnp.arange(0, HEAD_DIM, 2, dtype=F32) / HEAD_DIM)
    ang = jnp.arange(seq, dtype=F32)[:, None] * inv[None, :]
    ang = jnp.tile(ang, (1, LANES // (HEAD_DIM // 2)))
    first = (jnp.arange(LANES) % HEAD_DIM) < HEAD_DIM // 2
    sin = jnp.sin(ang)
    return jnp.cos(ang), jnp.where(first, -sin, 0.0), jnp.where(first, 0.0, sin)


def _inproj(x2, gain, w_bf, seq, hy3, aw, tm):
    t, d = x2.shape
    cos, sa, sb = _rope_tables(seq)
    nseq = seq // tm
    tab = pl.BlockSpec((tm, LANES), lambda i: (i % nseq, 0))
    row = lambda n: pl.BlockSpec((tm, n), lambda i: (i, 0))
    return pl.pallas_call(
        functools.partial(_inproj_kernel, hy3=hy3, aw=aw, scale=HEAD_DIM ** -0.5 * math.log2(math.e)),
        grid=(t // tm,),
        in_specs=[row(d), pl.BlockSpec((1, d), lambda i: (0, 0)),
                  pl.BlockSpec(w_bf.shape, lambda i: (0, 0)), tab, tab, tab],
        out_specs=[row(hy3), row(aw), row(aw), row(aw)],
        out_shape=[jax.ShapeDtypeStruct((t, hy3), BF16)] + [jax.ShapeDtypeStruct((t, aw), BF16)] * 3,
        compiler_params=_cparams(("parallel",), 48),
        name="inproj",
    )(x2, gain, w_bf, cos, sa, sb)


def _hygate_kernel(prev_ref, cur_ref, next_ref, w_ref, b_ref, g_ref, x0_ref, *, c):
    i = pl.program_id(1)
    last = pl.num_programs(1) - 1
    u = cur_ref[0].astype(F32)
    tl = u.shape[0]
    prev_row = prev_ref[0, BF16_SUBLANES - 1:BF16_SUBLANES, :].astype(F32) * jnp.where(i > 0, 1.0, 0.0)
    next_row = next_ref[0, 0:1, :].astype(F32) * jnp.where(i < last, 1.0, 0.0)
    rows = lax.broadcasted_iota(jnp.int32, u.shape, 0)
    up = jnp.where(rows == 0, prev_row, pltpu.roll(u, 1, 0))
    dn = jnp.where(rows == tl - 1, next_row, pltpu.roll(u, tl - 1, 0))
    y = up * w_ref[0:1, :] + u * w_ref[1:2, :] + dn * w_ref[2:3, :] + b_ref[...]
    x0_ref[0] = y[:, :c].astype(BF16)
    g = y[:, 2 * c:] * y[:, c:2 * c]
    for t in range(c // LANES):
        g_ref[0, t] = g[:, t * LANES:(t + 1) * LANES]


def _hygate(u, conv_w, conv_b, tl):
    b, s, c3 = u.shape
    c = c3 // 3
    nct = c // LANES
    nb = tl // BF16_SUBLANES
    nhalo = s // BF16_SUBLANES
    return pl.pallas_call(
        functools.partial(_hygate_kernel, c=c),
        grid=(b, s // tl),
        in_specs=[
            pl.BlockSpec((1, BF16_SUBLANES, c3), lambda bi, i: (bi, jnp.maximum(i * nb - 1, 0), 0)),
            pl.BlockSpec((1, tl, c3), lambda bi, i: (bi, i, 0)),
            pl.BlockSpec((1, BF16_SUBLANES, c3), lambda bi, i: (bi, jnp.minimum((i + 1) * nb, nhalo - 1), 0)),
            pl.BlockSpec((3, c3), lambda bi, i: (0, 0)),
            pl.BlockSpec((1, c3), lambda bi, i: (0, 0)),
        ],
        out_specs=[pl.BlockSpec((1, nct, tl, LANES), lambda bi, i: (bi, 0, i, 0)),
                   pl.BlockSpec((1, tl, c), lambda bi, i: (bi, i, 0))],
        out_shape=[jax.ShapeDtypeStruct((b, nct, s, LANES), F32), jax.ShapeDtypeStruct((b, s, c), BF16)],
        compiler_params=_cparams(("parallel", "parallel"), 48),
        name="hygate",
    )(u, u, u, conv_w, conv_b)


def _filter_kernel(z_ref, dec_ref, w1_ref, b1_ref, w2_ref, b2_ref, w3_ref, b3_ref, w4_ref, fr_ref, o_ref, *, c):
    dot = functools.partial(jnp.dot, preferred_element_type=F32, precision=lax.Precision.HIGHEST)
    fr = fr_ref[...]
    h = jnp.sin(fr * (dot(z_ref[...], w1_ref[...]) + b1_ref[...]))
    h = jnp.sin(fr * (dot(h, w2_ref[...]) + b2_ref[...]))
    h = jnp.sin(fr * (dot(h, w3_ref[...]) + b3_ref[...]))
    h = dot(h, w4_ref[...])
    dec = dec_ref[...]
    for d in range(2):
        hd = h[:, d * c:(d + 1) * c] * dec
        for t in range(c // LANES):
            o_ref[d, t] = hd[:, t * LANES:(t + 1) * LANES]


def _filter_inputs(seq, c):
    bands = (FILTER_EMB_DIM - 1) // 2
    t = jnp.linspace(0.0, 1.0, seq, dtype=F32)[:, None]
    w = (2.0 * math.pi / seq) * jnp.arange(seq, dtype=F32)[:, None]
    f = jnp.linspace(1e-4, bands - 1, bands, dtype=F32)[None, :]
    fw = f * w
    z = jnp.concatenate([t, jnp.cos(fw), -jnp.sin(fw)], axis=-1)
    max_decay = math.log(FILTER_TARGET) / FILTER_FAST_DECAY_PCT
    min_decay = math.log(FILTER_TARGET) / FILTER_SLOW_DECAY_PCT
    deltas = jnp.linspace(min_decay, max_decay, c, dtype=F32)[None, :]
    return z, jnp.exp(-t * jnp.abs(deltas))


def _filters(seq, c, w1, b1, w2, b2, w3, b3, w4, freq, ts):
    z, decay = _filter_inputs(seq, c)
    order = w1.shape[1]
    zp = jnp.pad(z, ((0, 0), (0, order - FILTER_EMB_DIM)))
    w1p = jnp.pad(w1, ((0, order - FILTER_EMB_DIM), (0, 0)))
    full = lambda a: pl.BlockSpec(a.shape, lambda i: (0,) * a.ndim)
    args = (zp, decay, w1p, b1, w2, b2, w3, b3, w4, freq)
    return pl.pallas_call(
        functools.partial(_filter_kernel, c=c),
        grid=(seq // ts,),
        in_specs=[pl.BlockSpec((ts, order), lambda i: (i, 0)), pl.BlockSpec((ts, c), lambda i: (i, 0))]
        + [full(a) for a in args[2:]],
        out_specs=pl.BlockSpec((2, c // LANES, ts, LANES), lambda i: (0, 0, i, 0)),
        out_shape=jax.ShapeDtypeStruct((2, c // LANES, seq, LANES), F32),
        compiler_params=_cparams(("parallel",), 48),
        name="hyfilter",
    )(*args)


def _cis(m, n):
    ang = (m % n).astype(F32) * (2.0 * math.pi / n)
    return jnp.cos(ang), jnp.sin(ang)


def _dft_tables(n1, n2):
    n = n1 * n2
    n1h = n1 // 2
    k1 = jnp.arange(n1, dtype=jnp.int32)[:, None]
    j1 = jnp.arange(n1h, dtype=jnp.int32)[None, :]
    c, s = _cis(k1 * j1, n1)
    fa = jnp.stack([c, -s], axis=1).reshape(2 * n1, n1h)
    fb = jnp.stack([s, c], axis=1).reshape(2 * n1, n1h)
    f1 = jnp.concatenate([fa, fb], axis=1)
    kk = (jnp.arange(n1, dtype=jnp.int32)[:, None, None] + n1 * jnp.arange(n2, dtype=jnp.int32)[None, :, None])
    j2 = jnp.arange(n2, dtype=jnp.int32)[None, None, :]
    c, s = _cis(kk * j2, n)
    m2 = jnp.concatenate([jnp.concatenate([c, s], axis=2), jnp.concatenate([-s, c], axis=2)], axis=1)
    ct, st = jnp.swapaxes(c, 1, 2), jnp.swapaxes(s, 1, 2)
    m2i = jnp.concatenate([jnp.concatenate([ct, -st], axis=2), jnp.concatenate([st, ct], axis=2)], axis=1)
    j1 = jnp.arange(n1h, dtype=jnp.int32)[:, None]
    k1 = jnp.arange(n1, dtype=jnp.int32)[None, :]
    c, s = _cis(j1 * k1, n1)
    c, s = c / n, s / n
    g_re = jnp.stack([c, -s], axis=2).reshape(n1h, 2 * n1)
    g_im = jnp.stack([s, c], axis=2).reshape(n1h, 2 * n1)
    g1 = jnp.concatenate([g_re, g_im], axis=0)
    return f1.astype(BF16), m2.astype(BF16), m2i.astype(BF16), g1.astype(BF16)


def _fft_split(seq):
    n1 = min(128, seq // 8)
    return n1, 2 * seq // n1


FFT_UNROLL = 8


def _hyspec_kernel(fa_ref, m2_ref, bias_ref, hf_ref, hb_ref, h_ref, af_ref, ab_ref, *, n1, n2):
    n1h = n1 // 2
    fa = fa_ref[...]

    def stage1(j2, carry):
        src = pl.ds(j2, n1h, stride=n2)
        dst = pl.ds(j2, 2 * n1, stride=n2)
        af_ref[dst, :] = jnp.dot(fa, hf_ref[0, 0, src, :].astype(BF16), preferred_element_type=F32)
        ab_ref[dst, :] = jnp.dot(fa, hb_ref[0, 0, src, :].astype(BF16), preferred_element_type=F32)
        return carry

    lax.fori_loop(0, n2, stage1, 0, unroll=min(FFT_UNROLL, n2))
    bias = bias_ref[...]

    def stage2(k1, carry):
        rows = pl.ds(pl.multiple_of(k1 * 2 * n2, 2 * n2), 2 * n2)
        xf = jnp.dot(m2_ref[k1], af_ref[rows, :].astype(BF16), preferred_element_type=F32)
        xb = jnp.dot(m2_ref[k1], ab_ref[rows, :].astype(BF16), preferred_element_type=F32)
        h_ref[k1, :n2, :] = (xf[:n2] + xb[:n2] + bias).astype(BF16)
        h_ref[k1, n2:, :] = (xf[n2:] - xb[n2:]).astype(BF16)
        return carry

    lax.fori_loop(0, n1, stage2, 0, unroll=min(FFT_UNROLL, n1))


def _hyconv_kernel(f1_ref, m2_ref, m2i_ref, g1_ref, h_ref, ga_ref, gb_ref, o_ref, a_ref, *, n1, n2):
    n1h = n1 // 2
    f1 = f1_ref[...]

    def stage1(j2, carry):
        src = pl.ds(j2, n1h, stride=n2)
        z = jnp.concatenate([ga_ref[0, 0, src, :], gb_ref[0, 0, src, :]], axis=0).astype(BF16)
        a_ref[pl.ds(j2, 2 * n1, stride=n2), :] = jnp.dot(f1, z, preferred_element_type=F32)
        return carry

    lax.fori_loop(0, n2, stage1, 0, unroll=min(FFT_UNROLL, n2))

    def stage2(k1, carry):
        rows = pl.ds(pl.multiple_of(k1 * 2 * n2, 2 * n2), 2 * n2)
        x = jnp.dot(m2_ref[k1], a_ref[rows, :].astype(BF16), preferred_element_type=F32)
        xr, xi = x[:n2], x[n2:]
        hr, hi = h_ref[k1, :n2, :].astype(F32), h_ref[k1, n2:, :].astype(F32)
        y = jnp.concatenate([xr * hr - xi * hi, xr * hi + xi * hr], axis=0).astype(BF16)
        a_ref[rows, :] = jnp.dot(m2i_ref[k1], y, preferred_element_type=F32)
        return carry

    lax.fori_loop(0, n1, stage2, 0, unroll=min(FFT_UNROLL, n1))
    g1 = g1_ref[...]

    def stage3(j2, carry):
        bs = a_ref[pl.ds(j2, 2 * n1, stride=n2), :].astype(BF16)
        y = jnp.dot(g1, bs, preferred_element_type=F32)
        dst = pl.ds(j2, n1h, stride=n2)
        o_ref[0, 0, 0, dst, :] = y[:n1h]
        o_ref[1, 0, 0, dst, :] = y[n1h:]
        return carry

    lax.fori_loop(0, n2, stage3, 0, unroll=min(FFT_UNROLL, n2))


def _long_conv(g, hfb, filt_bias):
    b, nct, seq, _ = g.shape
    pairs = b // 2
    n1, n2 = _fft_split(seq)
    f1, m2, m2i, g1 = _dft_tables(n1, n2)
    fa = f1[:, :n1 // 2]
    rows = 2 * n1 * n2
    const = lambda a, nd: pl.BlockSpec(a.shape, (lambda t: (0,) * a.ndim) if nd == 1 else (lambda t, p: (0,) * a.ndim),
                                       pipeline_mode=pl.Buffered(1))
    h = pl.pallas_call(
        functools.partial(_hyspec_kernel, n1=n1, n2=n2),
        grid=(nct,),
        in_specs=[const(fa, 1), const(m2, 1), pl.BlockSpec((1, LANES), lambda t: (0, t)),
                  pl.BlockSpec((1, 1, seq, LANES), lambda t: (0, t, 0, 0)),
                  pl.BlockSpec((1, 1, seq, LANES), lambda t: (1, t, 0, 0))],
        out_specs=pl.BlockSpec((n1, 2 * n2, LANES), lambda t: (0, 0, t)),
        out_shape=jax.ShapeDtypeStruct((n1, 2 * n2, nct * LANES), BF16),
        scratch_shapes=[pltpu.VMEM((rows, LANES), F32), pltpu.VMEM((rows, LANES), F32)],
        compiler_params=_cparams(("parallel",), 48),
        name="hyspectrum",
    )(fa, m2, filt_bias, hfb, hfb)
    seqblk = lambda off: pl.BlockSpec((1, 1, seq, LANES), lambda t, p: (p + off, t, 0, 0))
    y = pl.pallas_call(
        functools.partial(_hyconv_kernel, n1=n1, n2=n2),
        grid=(nct, pairs),
        in_specs=[const(f1, 2), const(m2, 2), const(m2i, 2), const(g1, 2),
                  pl.BlockSpec((n1, 2 * n2, LANES), lambda t, p: (0, 0, t)), seqblk(0), seqblk(pairs)],
        out_specs=pl.BlockSpec((2, 1, 1, seq, LANES), lambda t, p: (0, p, t, 0, 0)),
        out_shape=jax.ShapeDtypeStruct((2, pairs, nct, seq, LANES), F32),
        scratch_shapes=[pltpu.VMEM((rows, LANES), F32)],
        compiler_params=_cparams(("parallel", "parallel"), 48),
        name="hyconv",
    )(f1, m2, m2i, g1, h, g, g)
    return y.reshape(b, nct, seq, LANES)


def _lane_fold(x, op):
    acc = x[:, :LANES]
    for t in range(1, x.shape[1] // LANES):
        acc = op(acc, x[:, t * LANES:(t + 1) * LANES])
    return acc


def _row_fold(x, op):
    parts = [x[t * 8:(t + 1) * 8] for t in range(x.shape[0] // 8)]
    while len(parts) > 1:
        parts = [op(parts[t], parts[t + 1]) for t in range(0, len(parts) - 1, 2)] + parts[len(parts) & ~1:]
    return parts[0]


def _attn_kernel(lq1_ref, lk1_ref, lq2_ref, lk2_ref, sg_ref, q_ref, k_ref, v_ref, o_ref,
                 s_ref, p_ref, m_ref, l_ref, vt_ref, *, lam_init, qb, kc):
    seq = k_ref.shape[1]
    nk, nb = seq // kc, seq // qb
    lam = (jnp.exp(jnp.sum(lq1_ref[...] * lk1_ref[...], axis=-1, keepdims=True))
           - jnp.exp(jnp.sum(lq2_ref[...] * lk2_ref[...], axis=-1, keepdims=True)) + lam_init)
    lane = lax.broadcasted_iota(jnp.int32, (qb, HEAD_WIDTH), 1)
    zero = jnp.zeros((qb, HEAD_WIDTH), BF16)
    for j in range(nk):
        keys = slice(j * kc, (j + 1) * kc)
        vt_ref[:, keys] = v_ref[0, keys, :].astype(F32).T.astype(BF16)

    def rows(i):
        return pl.ds(pl.multiple_of(i * qb, qb), qb)

    def stage(i_values, i_exps, i_scores):
        if i_values is not None:
            l1 = jnp.sum(l_ref[0], axis=0, keepdims=True)
            l2 = jnp.sum(l_ref[1], axis=0, keepdims=True)
            ratio = (lam * l1 / l2).astype(BF16)
            ot = jnp.zeros((HEAD_WIDTH, qb), F32)
        if i_exps is not None:
            colmax = [jnp.max(m_ref[c], axis=0, keepdims=True) for c in range(2)]
            lacc = [jnp.zeros((8, qb), F32) for _ in range(2)]
        if i_scores is not None:
            q = q_ref[0, rows(i_scores), :]
            qc = (jnp.where(lane < HEAD_DIM, q, zero), jnp.where(lane < HEAD_DIM, zero, q))
            macc = [jnp.full((8, qb), -jnp.inf, F32) for _ in range(2)]
        for j in range(nk):
            keys = slice(j * kc, (j + 1) * kc)
            tie = None
            if i_exps is not None and j > 0:
                both = jnp.max(jnp.maximum(lacc[0], lacc[1]), axis=0, keepdims=True)
                tie = jnp.where(_lane_fold(both, jnp.maximum) > 1e30, 1.0, 0.0).astype(BF16)
            if i_values is not None:
                wt = p_ref[0, j] - p_ref[1, j] * ratio
                ot = ot + jnp.dot(vt_ref[:, keys], wt, preferred_element_type=F32)
            if i_exps is not None:
                for c in range(2):
                    e = jnp.exp2(s_ref[c, j] - colmax[c])
                    p_ref[c, j] = e.astype(BF16)
                    lacc[c] = lacc[c] + _row_fold(e, jnp.add)
            if i_scores is not None:
                for c in range(2):
                    qcj = qc[c] if tie is None else qc[c] + tie
                    s = lax.dot_general(k_ref[0, keys, :], qcj, (((1,), (1,)), ((), ())),
                                        preferred_element_type=F32)
                    s_ref[c, j] = s
                    macc[c] = jnp.maximum(macc[c], _row_fold(s, jnp.maximum))
        for c in range(2):
            if i_exps is not None:
                l_ref[c] = lacc[c]
            if i_scores is not None:
                m_ref[c] = macc[c]
        if i_values is not None:
            o = (ot * (1.0 / l1)).T
            o = _rms(o, SUBLN_EPS) * sg_ref[...] * (1.0 - lam_init)
            o_ref[0, rows(i_values), :] = o.astype(BF16)

    def steady(i, carry):
        stage(i - 1, i, i + 1)
        return carry

    stage(None, None, 0)
    stage(None, 0, 1)
    lax.fori_loop(1, nb - 1, steady, 0)
    stage(nb - 2, nb - 1, None)
    stage(nb - 1, None, None)


def _attention(q, k, v, lq1, lk1, lq2, lk2, subln_gain, lam_init, qb, kc):
    b, s, aw = q.shape
    heads = aw // HEAD_WIDTH
    assert s // qb >= 2
    vec = lambda a: pl.BlockSpec(a.shape, lambda bi, h: (0, 0))
    seqblk = pl.BlockSpec((1, s, HEAD_WIDTH), lambda bi, h: (bi, 0, h))
    return pl.pallas_call(
        functools.partial(_attn_kernel, lam_init=lam_init, qb=qb, kc=kc),
        grid=(b, heads),
        in_specs=[vec(lq1), vec(lk1), vec(lq2), vec(lk2), vec(subln_gain), seqblk, seqblk, seqblk],
        out_specs=seqblk,
        out_shape=jax.ShapeDtypeStruct((b, s, aw), BF16),
        scratch_shapes=[pltpu.VMEM((2, s // kc, kc, qb), F32), pltpu.VMEM((2, s // kc, kc, qb), BF16),
                        pltpu.VMEM((2, 8, qb), F32), pltpu.VMEM((2, 8, qb), F32), pltpu.VMEM((HEAD_WIDTH, s), BF16)],
        compiler_params=_cparams(("parallel", "parallel"), 48),
        name="diffattn",
    )(lq1, lk1, lq2, lk2, subln_gain, q, k, v)


def _outproj_kernel(yc_ref, x0_ref, ya_ref, w_ref, g_ref, x_ref, o_ref, *, c):
    conv = jnp.concatenate([yc_ref[0, t] for t in range(c // LANES)], axis=1)
    yh = (conv * x0_ref[0].astype(F32)).astype(BF16)
    mix = (jnp.dot(yh, w_ref[:c, :], preferred_element_type=F32)
           + jnp.dot(ya_ref[0], w_ref[c:, :], preferred_element_type=F32))
    o_ref[0] = x_ref[0] + _rms(mix, NORM_EPS) * g_ref[...]


def _outproj(yc, x0, ya, w_bf, gain, x, tm):
    b, s, d = x.shape
    c = x0.shape[2]
    row = lambda n: pl.BlockSpec((1, tm, n), lambda bi, i: (bi, i, 0))
    return pl.pallas_call(
        functools.partial(_outproj_kernel, c=c),
        grid=(b, s // tm),
        in_specs=[pl.BlockSpec((1, c // LANES, tm, LANES), lambda bi, i: (bi, 0, i, 0)), row(c), row(ya.shape[2]),
                  pl.BlockSpec(w_bf.shape, lambda bi, i: (0, 0)), pl.BlockSpec((1, d), lambda bi, i: (0, 0)), row(d)],
        out_specs=row(d),
        out_shape=jax.ShapeDtypeStruct((b, s, d), F32),
        compiler_params=_cparams(("parallel", "parallel"), 48),
        name="outproj",
    )(yc, x0, ya, w_bf, gain, x)


def _mlp_kernel(x_ref, gpre_ref, wu_ref, wd_ref, gpost_ref, o_ref, *, fc):
    x = x_ref[...]
    hb = (_rms(x, NORM_EPS) * gpre_ref[...]).astype(BF16)
    acc = jnp.zeros(x.shape, F32)
    for j in range(wu_ref.shape[1] // fc):
        up = jnp.dot(hb, wu_ref[:, j * fc:(j + 1) * fc], preferred_element_type=F32)
        act = jnp.square(jnp.maximum(up, 0.0)).astype(BF16)
        acc = acc + jnp.dot(act, wd_ref[j * fc:(j + 1) * fc, :], preferred_element_type=F32)
    o_ref[...] = x + _rms(acc, NORM_EPS) * gpost_ref[...]


def _mlp(x2, gpre, wu_bf, wd_bf, gpost, tm, fc):
    t, d = x2.shape
    row = pl.BlockSpec((tm, d), lambda i: (i, 0))
    vec = pl.BlockSpec((1, d), lambda i: (0, 0))
    resident = lambda a: pl.BlockSpec(a.shape, lambda i: (0, 0), pipeline_mode=pl.Buffered(1))
    return pl.pallas_call(
        functools.partial(_mlp_kernel, fc=fc),
        grid=(t // tm,),
        in_specs=[row, vec, resident(wu_bf), resident(wd_bf), vec],
        out_specs=row,
        out_shape=jax.ShapeDtypeStruct((t, d), F32),
        compiler_params=_cparams(("parallel",), 56),
        name="mlp",
    )(x2, gpre, wu_bf, wd_bf, gpost)


def _tile(n, want):
    return min(n, want)


def kernel(x, attn_pre_gain, attn_post_gain, w_in, conv_w, conv_b, filt_w1, filt_b1, filt_w2, filt_b2, filt_w3, filt_b3, filt_w4, filt_freq, filt_bias, lam_q1, lam_k1, lam_q2, lam_k2, subln_gain, w_out, mlp_pre_gain, mlp_post_gain, w_up, w_down):
    b, s, d = x.shape
    depth = w_in.shape[0]
    c = filt_bias.shape[1]
    hy3 = 3 * c
    aw = (w_in.shape[2] - hy3) // 3
    t = b * s
    tm = _tile(t, 512)
    row = lambda a, l: a[l][None, :]
    x2 = x.reshape(t, d)
    for l in range(depth):
        u, q, k, v = _inproj(x2, row(attn_pre_gain, l), w_in[l].astype(BF16), s, hy3, aw, _tile(s, 512))
        g, x0 = _hygate(u.reshape(b, s, hy3), conv_w[l], row(conv_b, l), _tile(s, 512))
        hfb = _filters(s, c, filt_w1[l], row(filt_b1, l), filt_w2[l], row(filt_b2, l), filt_w3[l],
                       row(filt_b3, l), filt_w4[l], row(filt_freq, l), _tile(s, 512))
        yc = _long_conv(g, hfb, row(filt_bias, l))
        lam_init = 0.8 - 0.6 * math.exp(-0.3 * l)
        y_at = _attention(q.reshape(b, s, aw), k.reshape(b, s, aw), v.reshape(b, s, aw),
                          row(lam_q1, l), row(lam_k1, l), row(lam_q2, l), row(lam_k2, l),
                          row(subln_gain, l), lam_init, _tile(s, 256), _tile(s, 512))
        x2 = _outproj(yc, x0, y_at, w_out[l].astype(BF16), row(attn_post_gain, l),
                      x2.reshape(b, s, d), _tile(s, 512)).reshape(t, d)
        x2 = _mlp(x2, row(mlp_pre_gain, l), w_up[l].astype(BF16), w_down[l].astype(BF16),
                  row(mlp_post_gain, l), tm, _tile(w_up.shape[2], 1024))
    return x2.reshape(b, s, d)
```

```python
import functools
import math

import jax
import jax.numpy as jnp
from jax import lax
from jax.experimental import pallas as pl
from jax.experimental.pallas import tpu as pltpu

F32 = jnp.float32
BF16 = jnp.bfloat16

HEAD_DIM = 64
HEAD_WIDTH = 2 * HEAD_DIM
FILTER_EMB_DIM = 33
FILTER_TARGET = 1e-2
FILTER_FAST_DECAY_PCT = 0.3
FILTER_SLOW_DECAY_PCT = 1.5
ROPE_THETA = 10000.0
NORM_EPS = 1e-6
SUBLN_EPS = 1e-5

LANES = 128
BF16_SUBLANES = 16
MIB = 1024 * 1024


def _cparams(semantics, vmem_mib):
    return pltpu.CompilerParams(dimension_semantics=semantics, vmem_limit_bytes=vmem_mib * MIB)


def _rms(x, eps):
    return x * lax.rsqrt(jnp.mean(x * x, axis=-1, keepdims=True) + eps)


def _fft_split(seq):
    n1 = min(128, seq // 8)
    return n1, 2 * seq // n1


ROW_PAD = 8


def _padded_rows(rows, seq):
    n2 = _fft_split(seq)[1]
    return rows // n2 * (n2 + ROW_PAD)


def _store_groups(ref, lead, x, n2):
    pitch = n2 + ROW_PAD
    pad = jnp.zeros((ROW_PAD, x.shape[1]), x.dtype)
    for gi in range(x.shape[0] // n2):
        ref[lead + (slice(gi * pitch, gi * pitch + n2),)] = x[gi * n2:(gi + 1) * n2]
        ref[lead + (slice(gi * pitch + n2, (gi + 1) * pitch),)] = pad


def _inproj_kernel(x_ref, g_ref, w_ref, cos_ref, sa_ref, sb_ref, u_ref, q_ref, k_ref, v_ref, *, hy3, aw, scale):
    x = x_ref[...]
    hb = (_rms(x, NORM_EPS) * g_ref[...]).astype(BF16)
    u_ref[...] = jnp.dot(hb, w_ref[:, :hy3], preferred_element_type=F32).astype(BF16)
    cos, sa, sb = cos_ref[...], sa_ref[...], sb_ref[...]

    def rope(t):
        return t * cos + pltpu.roll(t, LANES - HEAD_DIM // 2, 1) * sa + pltpu.roll(t, HEAD_DIM // 2, 1) * sb

    q = jnp.dot(hb, w_ref[:, hy3:hy3 + aw], preferred_element_type=F32)
    k = jnp.dot(hb, w_ref[:, hy3 + aw:hy3 + 2 * aw], preferred_element_type=F32)
    for j in range(aw // LANES):
        sl = slice(j * LANES, (j + 1) * LANES)
        q_ref[:, sl] = (rope(q[:, sl]) * scale).astype(BF16)
        k_ref[:, sl] = rope(k[:, sl]).astype(BF16)
    v_ref[...] = jnp.dot(hb, w_ref[:, hy3 + 2 * aw:], preferred_element_type=F32).astype(BF16)


def _rope_tables(seq):
    inv = ROPE_THETA ** (-jnp.arange(0, HEAD_DIM, 2, dtype=F32) / HEAD_DIM)
    ang = jnp.arange(seq, dtype=F32)[:, None] * inv[None, :]
    ang = jnp.tile(ang, (1, LANES // (HEAD_DIM // 2)))
    first = (jnp.arange(LANES) % HEAD_DIM) < HEAD_DIM // 2
    sin = jnp.sin(ang)
    return jnp.cos(ang), jnp.where(first, -sin, 0.0), jnp.where(first, 0.0, sin)


def _inproj(x2, gain, w_bf, seq, hy3, aw, tm):
    t, d = x2.shape
    cos, sa, sb = _rope_tables(seq)
    nseq = seq // tm
    tab = pl.BlockSpec((tm, LANES), lambda i: (i % nseq, 0))
    row = lambda n: pl.BlockSpec((tm, n), lambda i: (i, 0))
    return pl.pallas_call(
        functools.partial(_inproj_kernel, hy3=hy3, aw=aw, scale=HEAD_DIM ** -0.5 * math.log2(math.e)),
        grid=(t // tm,),
        in_specs=[row(d), pl.BlockSpec((1, d), lambda i: (0, 0)),
                  pl.BlockSpec(w_bf.shape, lambda i: (0, 0)), tab, tab, tab],
        out_specs=[row(hy3), row(aw), row(aw), row(aw)],
        out_shape=[jax.ShapeDtypeStruct((t, hy3), BF16)] + [jax.ShapeDtypeStruct((t, aw), BF16)] * 3,
        compiler_params=_cparams(("parallel",), 48),
        name="inproj",
    )(x2, gain, w_bf, cos, sa, sb)


def _hygate_kernel(prev_ref, cur_ref, next_ref, w_ref, b_ref, g_ref, x0_ref, *, c, n2):
    i = pl.program_id(1)
    last = pl.num_programs(1) - 1
    u = cur_ref[0].astype(F32)
    tl = u.shape[0]
    prev_row = prev_ref[0, BF16_SUBLANES - 1:BF16_SUBLANES, :].astype(F32) * jnp.where(i > 0, 1.0, 0.0)
    next_row = next_ref[0, 0:1, :].astype(F32) * jnp.where(i < last, 1.0, 0.0)
    rows = lax.broadcasted_iota(jnp.int32, u.shape, 0)
    up = jnp.where(rows == 0, prev_row, pltpu.roll(u, 1, 0))
    dn = jnp.where(rows == tl - 1, next_row, pltpu.roll(u, tl - 1, 0))
    y = up * w_ref[0:1, :] + u * w_ref[1:2, :] + dn * w_ref[2:3, :] + b_ref[...]
    x0_ref[0] = y[:, :c].astype(BF16)
    g = y[:, 2 * c:] * y[:, c:2 * c]
    for t in range(c // LANES):
        _store_groups(g_ref, (0, t), g[:, t * LANES:(t + 1) * LANES], n2)


def _hygate(u, conv_w, conv_b, tl):
    b, s, c3 = u.shape
    c = c3 // 3
    nct = c // LANES
    nb = tl // BF16_SUBLANES
    nhalo = s // BF16_SUBLANES
    return pl.pallas_call(
        functools.partial(_hygate_kernel, c=c, n2=_fft_split(s)[1]),
        grid=(b, s // tl),
        in_specs=[
            pl.BlockSpec((1, BF16_SUBLANES, c3), lambda bi, i: (bi, jnp.maximum(i * nb - 1, 0), 0)),
            pl.BlockSpec((1, tl, c3), lambda bi, i: (bi, i, 0)),
            pl.BlockSpec((1, BF16_SUBLANES, c3), lambda bi, i: (bi, jnp.minimum((i + 1) * nb, nhalo - 1), 0)),
            pl.BlockSpec((3, c3), lambda bi, i: (0, 0)),
            pl.BlockSpec((1, c3), lambda bi, i: (0, 0)),
        ],
        out_specs=[pl.BlockSpec((1, nct, _padded_rows(tl, s), LANES), lambda bi, i: (bi, 0, i, 0)),
                   pl.BlockSpec((1, tl, c), lambda bi, i: (bi, i, 0))],
        out_shape=[jax.ShapeDtypeStruct((b, nct, _padded_rows(s, s), LANES), F32),
                   jax.ShapeDtypeStruct((b, s, c), BF16)],
        compiler_params=_cparams(("parallel", "parallel"), 48),
        name="hygate",
    )(u, u, u, conv_w, conv_b)


def _filter_kernel(z_ref, dec_ref, w1_ref, b1_ref, w2_ref, b2_ref, w3_ref, b3_ref, w4_ref, fr_ref, o_ref, *, c, n2):
    dot = functools.partial(jnp.dot, preferred_element_type=F32, precision=lax.Precision.HIGHEST)
    fr = fr_ref[...]
    h = jnp.sin(fr * (dot(z_ref[...], w1_ref[...]) + b1_ref[...]))
    h = jnp.sin(fr * (dot(h, w2_ref[...]) + b2_ref[...]))
    h = jnp.sin(fr * (dot(h, w3_ref[...]) + b3_ref[...]))
    h = dot(h, w4_ref[...])
    dec = dec_ref[...]
    for d in range(2):
        hd = h[:, d * c:(d + 1) * c] * dec
        for t in range(c // LANES):
            _store_groups(o_ref, (d, t), hd[:, t * LANES:(t + 1) * LANES], n2)


def _filter_inputs(seq, c):
    bands = (FILTER_EMB_DIM - 1) // 2
    t = jnp.linspace(0.0, 1.0, seq, dtype=F32)[:, None]
    w = (2.0 * math.pi / seq) * jnp.arange(seq, dtype=F32)[:, None]
    f = jnp.linspace(1e-4, bands - 1, bands, dtype=F32)[None, :]
    fw = f * w
    z = jnp.concatenate([t, jnp.cos(fw), -jnp.sin(fw)], axis=-1)
    max_decay = math.log(FILTER_TARGET) / FILTER_FAST_DECAY_PCT
    min_decay = math.log(FILTER_TARGET) / FILTER_SLOW_DECAY_PCT
    deltas = jnp.linspace(min_decay, max_decay, c, dtype=F32)[None, :]
    return z, jnp.exp(-t * jnp.abs(deltas))


def _filters(seq, c, w1, b1, w2, b2, w3, b3, w4, freq, ts):
    z, decay = _filter_inputs(seq, c)
    order = w1.shape[1]
    zp = jnp.pad(z, ((0, 0), (0, order - FILTER_EMB_DIM)))
    w1p = jnp.pad(w1, ((0, order - FILTER_EMB_DIM), (0, 0)))
    full = lambda a: pl.BlockSpec(a.shape, lambda i: (0,) * a.ndim)
    args = (zp, decay, w1p, b1, w2, b2, w3, b3, w4, freq)
    return pl.pallas_call(
        functools.partial(_filter_kernel, c=c, n2=_fft_split(seq)[1]),
        grid=(seq // ts,),
        in_specs=[pl.BlockSpec((ts, order), lambda i: (i, 0)), pl.BlockSpec((ts, c), lambda i: (i, 0))]
        + [full(a) for a in args[2:]],
        out_specs=pl.BlockSpec((2, c // LANES, _padded_rows(ts, seq), LANES), lambda i: (0, 0, i, 0)),
        out_shape=jax.ShapeDtypeStruct((2, c // LANES, _padded_rows(seq, seq), LANES), F32),
        compiler_params=_cparams(("parallel",), 48),
        name="hyfilter",
    )(*args)


def _cis(m, n):
    ang = (m % n).astype(F32) * (2.0 * math.pi / n)
    return jnp.cos(ang), jnp.sin(ang)


def _dft_tables(n1, n2):
    n = n1 * n2
    n1h = n1 // 2
    k1 = jnp.arange(n1, dtype=jnp.int32)[:, None]
    j1 = jnp.arange(n1h, dtype=jnp.int32)[None, :]
    c, s = _cis(k1 * j1, n1)
    fa = jnp.stack([c, -s], axis=1).reshape(2 * n1, n1h)
    fb = jnp.stack([s, c], axis=1).reshape(2 * n1, n1h)
    f1 = jnp.concatenate([fa, fb], axis=1)
    kk = (jnp.arange(n1, dtype=jnp.int32)[:, None, None] + n1 * jnp.arange(n2, dtype=jnp.int32)[None, :, None])
    j2 = jnp.arange(n2, dtype=jnp.int32)[None, None, :]
    c, s = _cis(kk * j2, n)
    m2 = jnp.concatenate([jnp.concatenate([c, s], axis=2), jnp.concatenate([-s, c], axis=2)], axis=1)
    ct, st = jnp.swapaxes(c, 1, 2), jnp.swapaxes(s, 1, 2)
    m2i = jnp.concatenate([jnp.concatenate([ct, -st], axis=2), jnp.concatenate([st, ct], axis=2)], axis=1)
    j1 = jnp.arange(n1h, dtype=jnp.int32)[:, None]
    k1 = jnp.arange(n1, dtype=jnp.int32)[None, :]
    c, s = _cis(j1 * k1, n1)
    c, s = c / n, s / n
    g_re = jnp.stack([c, -s], axis=2).reshape(n1h, 2 * n1)
    g_im = jnp.stack([s, c], axis=2).reshape(n1h, 2 * n1)
    g1 = jnp.concatenate([g_re, g_im], axis=0)
    return f1.astype(BF16), m2.astype(BF16), m2i.astype(BF16), g1.astype(BF16)


FFT_UNROLL = 8


def _re_im_rows(k1, n2):
    pitch = n2 + ROW_PAD
    return (pl.ds(pl.multiple_of(k1 * 2 * pitch, 8), n2), pl.ds(pl.multiple_of(k1 * 2 * pitch + pitch, 8), n2))


def _hyspec_kernel(fa_ref, m2_ref, bias_ref, hf_ref, hb_ref, h_ref, af_ref, ab_ref, *, n1, n2):
    n1h = n1 // 2
    pitch = n2 + ROW_PAD
    fa = fa_ref[...]

    def stage1(j2, carry):
        src = pl.ds(j2, n1h, stride=pitch)
        dst = pl.ds(j2, 2 * n1, stride=pitch)
        af_ref[dst, :] = jnp.dot(fa, hf_ref[0, 0, src, :].astype(BF16), preferred_element_type=F32)
        ab_ref[dst, :] = jnp.dot(fa, hb_ref[0, 0, src, :].astype(BF16), preferred_element_type=F32)
        return carry

    lax.fori_loop(0, n2, stage1, 0, unroll=min(FFT_UNROLL, n2))
    bias = bias_ref[...]

    def stage2(k1, carry):
        re, im = _re_im_rows(k1, n2)
        sf = jnp.concatenate([af_ref[re, :], af_ref[im, :]], axis=0).astype(BF16)
        sb = jnp.concatenate([ab_ref[re, :], ab_ref[im, :]], axis=0).astype(BF16)
        xf = jnp.dot(m2_ref[k1], sf, preferred_element_type=F32)
        xb = jnp.dot(m2_ref[k1], sb, preferred_element_type=F32)
        h_ref[k1, :n2, :] = (xf[:n2] + xb[:n2] + bias).astype(BF16)
        h_ref[k1, n2:, :] = (xf[n2:] - xb[n2:]).astype(BF16)
        return carry

    lax.fori_loop(0, n1, stage2, 0, unroll=min(FFT_UNROLL, n1))


def _hyconv_kernel(f1_ref, m2_ref, m2i_ref, g1_ref, h_ref, ga_ref, gb_ref, o_ref, a_ref, *, n1, n2):
    n1h = n1 // 2
    pitch = n2 + ROW_PAD
    f1 = f1_ref[...]
    pad = jnp.zeros((ROW_PAD, LANES), F32)
    for r in range(2):
        for gi in range(n1h):
            o_ref[r, 0, 0, gi * pitch + n2:(gi + 1) * pitch, :] = pad

    def stage1(j2, carry):
        src = pl.ds(j2, n1h, stride=pitch)
        z = jnp.concatenate([ga_ref[0, 0, src, :], gb_ref[0, 0, src, :]], axis=0).astype(BF16)
        a_ref[pl.ds(j2, 2 * n1, stride=pitch), :] = jnp.dot(f1, z, preferred_element_type=F32)
        return carry

    lax.fori_loop(0, n2, stage1, 0, unroll=min(FFT_UNROLL, n2))

    def stage2(k1, carry):
        re, im = _re_im_rows(k1, n2)
        a = jnp.concatenate([a_ref[re, :], a_ref[im, :]], axis=0).astype(BF16)
        x = jnp.dot(m2_ref[k1], a, preferred_element_type=F32)
        xr, xi = x[:n2], x[n2:]
        hr, hi = h_ref[k1, :n2, :].astype(F32), h_ref[k1, n2:, :].astype(F32)
        y = jnp.concatenate([xr * hr - xi * hi, xr * hi + xi * hr], axis=0).astype(BF16)
        b = jnp.dot(m2i_ref[k1], y, preferred_element_type=F32)
        a_ref[re, :] = b[:n2]
        a_ref[im, :] = b[n2:]
        return carry

    lax.fori_loop(0, n1, stage2, 0, unroll=min(FFT_UNROLL, n1))
    g1 = g1_ref[...]

    def stage3(j2, carry):
        bs = a_ref[pl.ds(j2, 2 * n1, stride=pitch), :].astype(BF16)
        y = jnp.dot(g1, bs, preferred_element_type=F32)
        dst = pl.ds(j2, n1h, stride=pitch)
        o_ref[0, 0, 0, dst, :] = y[:n1h]
        o_ref[1, 0, 0, dst, :] = y[n1h:]
        return carry

    lax.fori_loop(0, n2, stage3, 0, unroll=min(FFT_UNROLL, n2))


def _long_conv(g, hfb, filt_bias, seq):
    b, nct, seqp, _ = g.shape
    pairs = b // 2
    n1, n2 = _fft_split(seq)
    f1, m2, m2i, g1 = _dft_tables(n1, n2)
    fa = f1[:, :n1 // 2]
    rows = 2 * n1 * (n2 + ROW_PAD)
    const = lambda a, nd: pl.BlockSpec(a.shape, (lambda t: (0,) * a.ndim) if nd == 1 else (lambda t, p: (0,) * a.ndim),
                                       pipeline_mode=pl.Buffered(1))
    h = pl.pallas_call(
        functools.partial(_hyspec_kernel, n1=n1, n2=n2),
        grid=(nct,),
        in_specs=[const(fa, 1), const(m2, 1), pl.BlockSpec((1, LANES), lambda t: (0, t)),
                  pl.BlockSpec((1, 1, seqp, LANES), lambda t: (0, t, 0, 0)),
                  pl.BlockSpec((1, 1, seqp, LANES), lambda t: (1, t, 0, 0))],
        out_specs=pl.BlockSpec((n1, 2 * n2, LANES), lambda t: (0, 0, t)),
        out_shape=jax.ShapeDtypeStruct((n1, 2 * n2, nct * LANES), BF16),
        scratch_shapes=[pltpu.VMEM((rows, LANES), F32), pltpu.VMEM((rows, LANES), F32)],
        compiler_params=_cparams(("parallel",), 48),
        name="hyspectrum",
    )(fa, m2, filt_bias, hfb, hfb)
    seqblk = lambda off: pl.BlockSpec((1, 1, seqp, LANES), lambda t, p: (p + off, t, 0, 0))
    y = pl.pallas_call(
        functools.partial(_hyconv_kernel, n1=n1, n2=n2),
        grid=(nct, pairs),
        in_specs=[const(f1, 2), const(m2, 2), const(m2i, 2), const(g1, 2),
                  pl.BlockSpec((n1, 2 * n2, LANES), lambda t, p: (0, 0, t)), seqblk(0), seqblk(pairs)],
        out_specs=pl.BlockSpec((2, 1, 1, seqp, LANES), lambda t, p: (0, p, t, 0, 0)),
        out_shape=jax.ShapeDtypeStruct((2, pairs, nct, seqp, LANES), F32),
        scratch_shapes=[pltpu.VMEM((rows, LANES), F32)],
        compiler_params=_cparams(("parallel", "parallel"), 48),
        name="hyconv",
    )(f1, m2, m2i, g1, h, g, g)
    return y.reshape(b, nct, seqp, LANES)


def _lane_fold(x, op):
    acc = x[:, :LANES]
    for t in range(1, x.shape[1] // LANES):
        acc = op(acc, x[:, t * LANES:(t + 1) * LANES])
    return acc


def _row_fold(x, op):
    parts = [x[t * 8:(t + 1) * 8] for t in range(x.shape[0] // 8)]
    while len(parts) > 1:
        parts = [op(parts[t], parts[t + 1]) for t in range(0, len(parts) - 1, 2)] + parts[len(parts) & ~1:]
    return parts[0]


def _attn_kernel(lq1_ref, lk1_ref, lq2_ref, lk2_ref, sg_ref, q_ref, k_ref, v_ref, o_ref,
                 s_ref, p_ref, m_ref, l_ref, vt_ref, *, lam_init, qb, kc):
    seq = k_ref.shape[1]
    nk, nb = seq // kc, seq // qb
    lam = (jnp.exp(jnp.sum(lq1_ref[...] * lk1_ref[...], axis=-1, keepdims=True))
           - jnp.exp(jnp.sum(lq2_ref[...] * lk2_ref[...], axis=-1, keepdims=True)) + lam_init)
    lane = lax.broadcasted_iota(jnp.int32, (qb, HEAD_WIDTH), 1)
    zero = jnp.zeros((qb, HEAD_WIDTH), BF16)
    for j in range(nk):
        keys = slice(j * kc, (j + 1) * kc)
        vt_ref[:, keys] = v_ref[0, keys, :].astype(F32).T.astype(BF16)

    def rows(i):
        return pl.ds(pl.multiple_of(i * qb, qb), qb)

    def stage(i_values, i_exps, i_scores):
        if i_values is not None:
            l1 = jnp.sum(l_ref[0], axis=0, keepdims=True)
            l2 = jnp.sum(l_ref[1], axis=0, keepdims=True)
            ratio = (lam * l1 / l2).astype(BF16)
            ot = jnp.zeros((HEAD_WIDTH, qb), F32)
        if i_exps is not None:
            colmax = [jnp.max(m_ref[c], axis=0, keepdims=True) for c in range(2)]
            lacc = [jnp.zeros((8, qb), F32) for _ in range(2)]
        if i_scores is not None:
            q = q_ref[0, rows(i_scores), :]
            qc = (jnp.where(lane < HEAD_DIM, q, zero), jnp.where(lane < HEAD_DIM, zero, q))
            macc = [jnp.full((8, qb), -jnp.inf, F32) for _ in range(2)]
        for j in range(nk):
            keys = slice(j * kc, (j + 1) * kc)
            tie = None
            if i_exps is not None and j > 0:
                both = jnp.max(jnp.maximum(lacc[0], lacc[1]), axis=0, keepdims=True)
                tie = jnp.where(_lane_fold(both, jnp.maximum) > 1e30, 1.0, 0.0).astype(BF16)
            if i_values is not None:
                wt = p_ref[0, j] - p_ref[1, j] * ratio
                ot = ot + jnp.dot(vt_ref[:, keys], wt, preferred_element_type=F32)
            if i_exps is not None:
                for c in range(2):
                    e = jnp.exp2(s_ref[c, j] - colmax[c])
                    p_ref[c, j] = e.astype(BF16)
                    lacc[c] = lacc[c] + _row_fold(e, jnp.add)
            if i_scores is not None:
                for c in range(2):
                    qcj = qc[c] if tie is None else qc[c] + tie
                    s = lax.dot_general(k_ref[0, keys, :], qcj, (((1,), (1,)), ((), ())),
                                        preferred_element_type=F32)
                    s_ref[c, j] = s
                    macc[c] = jnp.maximum(macc[c], _row_fold(s, jnp.maximum))
        for c in range(2):
            if i_exps is not None:
                l_ref[c] = lacc[c]
            if i_scores is not None:
                m_ref[c] = macc[c]
        if i_values is not None:
            o = (ot * (1.0 / l1)).T
            o = _rms(o, SUBLN_EPS) * sg_ref[...] * (1.0 - lam_init)
            o_ref[0, rows(i_values), :] = o.astype(BF16)

    def steady(i, carry):
        stage(i - 1, i, i + 1)
        return carry

    stage(None, None, 0)
    stage(None, 0, 1)
    lax.fori_loop(1, nb - 1, steady, 0)
    stage(nb - 2, nb - 1, None)
    stage(nb - 1, None, None)


def _attention(q, k, v, lq1, lk1, lq2, lk2, subln_gain, lam_init, qb, kc):
    b, s, aw = q.shape
    heads = aw // HEAD_WIDTH
    assert s // qb >= 2
    vec = lambda a: pl.BlockSpec(a.shape, lambda bi, h: (0, 0))
    seqblk = pl.BlockSpec((1, s, HEAD_WIDTH), lambda bi, h: (bi, 0, h))
    return pl.pallas_call(
        functools.partial(_attn_kernel, lam_init=lam_init, qb=qb, kc=kc),
        grid=(b, heads),
        in_specs=[vec(lq1), vec(lk1), vec(lq2), vec(lk2), vec(subln_gain), seqblk, seqblk, seqblk],
        out_specs=seqblk,
        out_shape=jax.ShapeDtypeStruct((b, s, aw), BF16),
        scratch_shapes=[pltpu.VMEM((2, s // kc, kc, qb), F32), pltpu.VMEM((2, s // kc, kc, qb), BF16),
                        pltpu.VMEM((2, 8, qb), F32), pltpu.VMEM((2, 8, qb), F32), pltpu.VMEM((HEAD_WIDTH, s), BF16)],
        compiler_params=_cparams(("parallel", "parallel"), 48),
        name="diffattn",
    )(lq1, lk1, lq2, lk2, subln_gain, q, k, v)


def _outproj_kernel(yc_ref, x0_ref, ya_ref, w_ref, g_ref, x_ref, o_ref, *, c, n2):
    pitch = n2 + ROW_PAD
    groups = range(x0_ref.shape[1] // n2)
    conv = jnp.concatenate([jnp.concatenate([yc_ref[0, t, gi * pitch:gi * pitch + n2, :] for gi in groups], axis=0)
                            for t in range(c // LANES)], axis=1)
    yh = (conv * x0_ref[0].astype(F32)).astype(BF16)
    mix = (jnp.dot(yh, w_ref[:c, :], preferred_element_type=F32)
           + jnp.dot(ya_ref[0], w_ref[c:, :], preferred_element_type=F32))
    o_ref[0] = x_ref[0] + _rms(mix, NORM_EPS) * g_ref[...]


def _outproj(yc, x0, ya, w_bf, gain, x, tm):
    b, s, d = x.shape
    c = x0.shape[2]
    row = lambda n: pl.BlockSpec((1, tm, n), lambda bi, i: (bi, i, 0))
    return pl.pallas_call(
        functools.partial(_outproj_kernel, c=c, n2=_fft_split(s)[1]),
        grid=(b, s // tm),
        in_specs=[pl.BlockSpec((1, c // LANES, _padded_rows(tm, s), LANES), lambda bi, i: (bi, 0, i, 0)),
                  row(c), row(ya.shape[2]),
                  pl.BlockSpec(w_bf.shape, lambda bi, i: (0, 0)), pl.BlockSpec((1, d), lambda bi, i: (0, 0)), row(d)],
        out_specs=row(d),
        out_shape=jax.ShapeDtypeStruct((b, s, d), F32),
        compiler_params=_cparams(("parallel", "parallel"), 48),
        name="outproj",
    )(yc, x0, ya, w_bf, gain, x)


def _mlp_kernel(x_ref, gpre_ref, wu_ref, wd_ref, gpost_ref, o_ref, *, fc):
    x = x_ref[...]
    hb = (_rms(x, NORM_EPS) * gpre_ref[...]).astype(BF16)
    acc = jnp.zeros(x.shape, F32)
    for j in range(wu_ref.shape[1] // fc):
        up = jnp.dot(hb, wu_ref[:, j * fc:(j + 1) * fc], preferred_element_type=F32)
        act = jnp.square(jnp.maximum(up, 0.0)).astype(BF16)
        acc = acc + jnp.dot(act, wd_ref[j * fc:(j + 1) * fc, :], preferred_element_type=F32)
    o_ref[...] = x + _rms(acc, NORM_EPS) * gpost_ref[...]


def _mlp(x2, gpre, wu_bf, wd_bf, gpost, tm, fc):
    t, d = x2.shape
    row = pl.BlockSpec((tm, d), lambda i: (i, 0))
    vec = pl.BlockSpec((1, d), lambda i: (0, 0))
    resident = lambda a: pl.BlockSpec(a.shape, lambda i: (0, 0), pipeline_mode=pl.Buffered(1))
    return pl.pallas_call(
        functools.partial(_mlp_kernel, fc=fc),
        grid=(t // tm,),
        in_specs=[row, vec, resident(wu_bf), resident(wd_bf), vec],
        out_specs=row,
        out_shape=jax.ShapeDtypeStruct((t, d), F32),
        compiler_params=_cparams(("parallel",), 56),
        name="mlp",
    )(x2, gpre, wu_bf, wd_bf, gpost)


def _tile(n, want):
    return min(n, want)


def kernel(x, attn_pre_gain, attn_post_gain, w_in, conv_w, conv_b, filt_w1, filt_b1, filt_w2, filt_b2, filt_w3, filt_b3, filt_w4, filt_freq, filt_bias, lam_q1, lam_k1, lam_q2, lam_k2, subln_gain, w_out, mlp_pre_gain, mlp_post_gain, w_up, w_down):
    b, s, d = x.shape
    depth = w_in.shape[0]
    c = filt_bias.shape[1]
    hy3 = 3 * c
    aw = (w_in.shape[2] - hy3) // 3
    t = b * s
    tm = _tile(t, 512)
    row = lambda a, l: a[l][None, :]
    x2 = x.reshape(t, d)
    for l in range(depth):
        u, q, k, v = _inproj(x2, row(attn_pre_gain, l), w_in[l].astype(BF16), s, hy3, aw, _tile(s, 512))
        g, x0 = _hygate(u.reshape(b, s, hy3), conv_w[l], row(conv_b, l), _tile(s, 512))
        hfb = _filters(s, c, filt_w1[l], row(filt_b1, l), filt_w2[l], row(filt_b2, l), filt_w3[l],
                       row(filt_b3, l), filt_w4[l], row(filt_freq, l), _tile(s, 512))
        yc = _long_conv(g, hfb, row(filt_bias, l), s)
        lam_init = 0.8 - 0.6 * math.exp(-0.3 * l)
        y_at = _attention(q.reshape(b, s, aw), k.reshape(b, s, aw), v.reshape(b, s, aw),
                          row(lam_q1, l), row(lam_k1, l), row(lam_q2, l), row(lam_k2, l),
                          row(subln_gain, l), lam_init, _tile(s, 256), _tile(s, 512))
        x2 = _outproj(yc, x0, y_at, w_out[l].astype(BF16), row(attn_post_gain, l),
                      x2.reshape(b, s, d), _tile(s, 512)).reshape(t, d)
        x2 = _mlp(x2, row(mlp_pre_gain, l), w_up[l].astype(BF16), w_down[l].astype(BF16),
                  row(mlp_post_gain, l), tm, _tile(w_up.shape[2], 1024))
    return x2.reshape(b, s, d)
```

```python
import functools
import math

import jax
import jax.numpy as jnp
from jax import lax
from jax.experimental import pallas as pl
from jax.experimental.pallas import tpu as pltpu

F32 = jnp.float32
BF16 = jnp.bfloat16

HEAD_DIM = 64
HEAD_WIDTH = 2 * HEAD_DIM
FILTER_EMB_DIM = 33
FILTER_TARGET = 1e-2
FILTER_FAST_DECAY_PCT = 0.3
FILTER_SLOW_DECAY_PCT = 1.5
ROPE_THETA = 10000.0
NORM_EPS = 1e-6
SUBLN_EPS = 1e-5

LANES = 128
BF16_SUBLANES = 16
MIB = 1024 * 1024


def _cparams(semantics, vmem_mib):
    return pltpu.CompilerParams(dimension_semantics=semantics, vmem_limit_bytes=vmem_mib * MIB)


def _rms(x, eps):
    return x * lax.rsqrt(jnp.mean(x * x, axis=-1, keepdims=True) + eps)


def _fft_split(seq):
    n1 = min(128, seq // 8)
    return n1, 2 * seq // n1


ROW_PAD = 8


def _padded_rows(rows, seq):
    n2 = _fft_split(seq)[1]
    return rows // n2 * (n2 + ROW_PAD)


def _store_groups(ref, lead, x, n2):
    pitch = n2 + ROW_PAD
    pad = jnp.zeros((ROW_PAD, x.shape[1]), x.dtype)
    for gi in range(x.shape[0] // n2):
        ref[lead + (slice(gi * pitch, gi * pitch + n2),)] = x[gi * n2:(gi + 1) * n2]
        ref[lead + (slice(gi * pitch + n2, (gi + 1) * pitch),)] = pad


def _inproj_kernel(x_ref, g_ref, w_ref, cos_ref, sa_ref, sb_ref, u_ref, q_ref, k_ref, v_ref, *, hy3, aw, scale):
    x = x_ref[0]
    hb = (_rms(x, NORM_EPS) * g_ref[...]).astype(BF16)
    u_ref[0] = jnp.dot(hb, w_ref[:, :hy3], preferred_element_type=F32).astype(BF16)
    cos, sa, sb = cos_ref[...], sa_ref[...], sb_ref[...]

    def rope(t):
        return t * cos + pltpu.roll(t, LANES - HEAD_DIM // 2, 1) * sa + pltpu.roll(t, HEAD_DIM // 2, 1) * sb

    q = jnp.dot(hb, w_ref[:, hy3:hy3 + aw], preferred_element_type=F32)
    k = jnp.dot(hb, w_ref[:, hy3 + aw:hy3 + 2 * aw], preferred_element_type=F32)
    v = jnp.dot(hb, w_ref[:, hy3 + 2 * aw:], preferred_element_type=F32)
    for h in range(aw // HEAD_WIDTH):
        sl = slice(h * HEAD_WIDTH, (h + 1) * HEAD_WIDTH)
        q_ref[0, h] = (rope(q[:, sl]) * scale).astype(BF16)
        k_ref[0, h] = rope(k[:, sl]).astype(BF16)
        v_ref[0, h] = v[:, sl].astype(BF16)


def _rope_tables(seq):
    inv = ROPE_THETA ** (-jnp.arange(0, HEAD_DIM, 2, dtype=F32) / HEAD_DIM)
    ang = jnp.arange(seq, dtype=F32)[:, None] * inv[None, :]
    ang = jnp.tile(ang, (1, LANES // (HEAD_DIM // 2)))
    first = (jnp.arange(LANES) % HEAD_DIM) < HEAD_DIM // 2
    sin = jnp.sin(ang)
    return jnp.cos(ang), jnp.where(first, -sin, 0.0), jnp.where(first, 0.0, sin)


def _inproj(x, gain, w_bf, hy3, aw, tm):
    b, seq, d = x.shape
    heads = aw // HEAD_WIDTH
    cos, sa, sb = _rope_tables(seq)
    tab = pl.BlockSpec((tm, LANES), lambda bi, i: (i, 0))
    row = lambda n: pl.BlockSpec((1, tm, n), lambda bi, i: (bi, i, 0))
    head_major = pl.BlockSpec((1, heads, tm, HEAD_WIDTH), lambda bi, i: (bi, 0, i, 0))
    return pl.pallas_call(
        functools.partial(_inproj_kernel, hy3=hy3, aw=aw, scale=HEAD_DIM ** -0.5 * math.log2(math.e)),
        grid=(b, seq // tm),
        in_specs=[row(d), pl.BlockSpec((1, d), lambda bi, i: (0, 0)),
                  pl.BlockSpec(w_bf.shape, lambda bi, i: (0, 0)), tab, tab, tab],
        out_specs=[row(hy3), head_major, head_major, head_major],
        out_shape=[jax.ShapeDtypeStruct((b, seq, hy3), BF16)]
        + [jax.ShapeDtypeStruct((b, heads, seq, HEAD_WIDTH), BF16)] * 3,
        compiler_params=_cparams(("parallel", "parallel"), 48),
        name="inproj",
    )(x, gain, w_bf, cos, sa, sb)


def _hygate_kernel(prev_ref, cur_ref, next_ref, w_ref, b_ref, g_ref, x0_ref, *, c, n2):
    i = pl.program_id(1)
    last = pl.num_programs(1) - 1
    u = cur_ref[0].astype(F32)
    tl = u.shape[0]
    prev_row = prev_ref[0, BF16_SUBLANES - 1:BF16_SUBLANES, :].astype(F32) * jnp.where(i > 0, 1.0, 0.0)
    next_row = next_ref[0, 0:1, :].astype(F32) * jnp.where(i < last, 1.0, 0.0)
    rows = lax.broadcasted_iota(jnp.int32, u.shape, 0)
    up = jnp.where(rows == 0, prev_row, pltpu.roll(u, 1, 0))
    dn = jnp.where(rows == tl - 1, next_row, pltpu.roll(u, tl - 1, 0))
    y = up * w_ref[0:1, :] + u * w_ref[1:2, :] + dn * w_ref[2:3, :] + b_ref[...]
    x0_ref[0] = y[:, :c].astype(BF16)
    g = y[:, 2 * c:] * y[:, c:2 * c]
    for t in range(c // LANES):
        _store_groups(g_ref, (0, t), g[:, t * LANES:(t + 1) * LANES], n2)


def _hygate(u, conv_w, conv_b, tl):
    b, s, c3 = u.shape
    c = c3 // 3
    nct = c // LANES
    nb = tl // BF16_SUBLANES
    nhalo = s // BF16_SUBLANES
    return pl.pallas_call(
        functools.partial(_hygate_kernel, c=c, n2=_fft_split(s)[1]),
        grid=(b, s // tl),
        in_specs=[
            pl.BlockSpec((1, BF16_SUBLANES, c3), lambda bi, i: (bi, jnp.maximum(i * nb - 1, 0), 0)),
            pl.BlockSpec((1, tl, c3), lambda bi, i: (bi, i, 0)),
            pl.BlockSpec((1, BF16_SUBLANES, c3), lambda bi, i: (bi, jnp.minimum((i + 1) * nb, nhalo - 1), 0)),
            pl.BlockSpec((3, c3), lambda bi, i: (0, 0)),
            pl.BlockSpec((1, c3), lambda bi, i: (0, 0)),
        ],
        out_specs=[pl.BlockSpec((1, nct, _padded_rows(tl, s), LANES), lambda bi, i: (bi, 0, i, 0)),
                   pl.BlockSpec((1, tl, c), lambda bi, i: (bi, i, 0))],
        out_shape=[jax.ShapeDtypeStruct((b, nct, _padded_rows(s, s), LANES), F32),
                   jax.ShapeDtypeStruct((b, s, c), BF16)],
        compiler_params=_cparams(("parallel", "parallel"), 48),
        name="hygate",
    )(u, u, u, conv_w, conv_b)


def _filter_kernel(z_ref, dec_ref, w1_ref, b1_ref, w2_ref, b2_ref, w3_ref, b3_ref, w4_ref, fr_ref, o_ref, *, c, n2):
    dot = functools.partial(jnp.dot, preferred_element_type=F32, precision=lax.Precision.HIGHEST)
    fr = fr_ref[...]
    h = jnp.sin(fr * (dot(z_ref[...], w1_ref[...]) + b1_ref[...]))
    h = jnp.sin(fr * (dot(h, w2_ref[...]) + b2_ref[...]))
    h = jnp.sin(fr * (dot(h, w3_ref[...]) + b3_ref[...]))
    h = dot(h, w4_ref[...])
    dec = dec_ref[...]
    for d in range(2):
        hd = h[:, d * c:(d + 1) * c] * dec
        for t in range(c // LANES):
            _store_groups(o_ref, (d, t), hd[:, t * LANES:(t + 1) * LANES], n2)


def _filter_inputs(seq, c):
    bands = (FILTER_EMB_DIM - 1) // 2
    t = jnp.linspace(0.0, 1.0, seq, dtype=F32)[:, None]
    w = (2.0 * math.pi / seq) * jnp.arange(seq, dtype=F32)[:, None]
    f = jnp.linspace(1e-4, bands - 1, bands, dtype=F32)[None, :]
    fw = f * w
    z = jnp.concatenate([t, jnp.cos(fw), -jnp.sin(fw)], axis=-1)
    max_decay = math.log(FILTER_TARGET) / FILTER_FAST_DECAY_PCT
    min_decay = math.log(FILTER_TARGET) / FILTER_SLOW_DECAY_PCT
    deltas = jnp.linspace(min_decay, max_decay, c, dtype=F32)[None, :]
    return z, jnp.exp(-t * jnp.abs(deltas))


def _filters(seq, c, w1, b1, w2, b2, w3, b3, w4, freq, ts):
    z, decay = _filter_inputs(seq, c)
    order = w1.shape[1]
    zp = jnp.pad(z, ((0, 0), (0, order - FILTER_EMB_DIM)))
    w1p = jnp.pad(w1, ((0, order - FILTER_EMB_DIM), (0, 0)))
    full = lambda a: pl.BlockSpec(a.shape, lambda i: (0,) * a.ndim)
    args = (zp, decay, w1p, b1, w2, b2, w3, b3, w4, freq)
    return pl.pallas_call(
        functools.partial(_filter_kernel, c=c, n2=_fft_split(seq)[1]),
        grid=(seq // ts,),
        in_specs=[pl.BlockSpec((ts, order), lambda i: (i, 0)), pl.BlockSpec((ts, c), lambda i: (i, 0))]
        + [full(a) for a in args[2:]],
        out_specs=pl.BlockSpec((2, c // LANES, _padded_rows(ts, seq), LANES), lambda i: (0, 0, i, 0)),
        out_shape=jax.ShapeDtypeStruct((2, c // LANES, _padded_rows(seq, seq), LANES), F32),
        compiler_params=_cparams(("parallel",), 48),
        name="hyfilter",
    )(*args)


def _cis(m, n):
    ang = (m % n).astype(F32) * (2.0 * math.pi / n)
    return jnp.cos(ang), jnp.sin(ang)


def _dft_tables(n1, n2):
    n = n1 * n2
    n1h = n1 // 2
    k1 = jnp.arange(n1, dtype=jnp.int32)[:, None]
    j1 = jnp.arange(n1h, dtype=jnp.int32)[None, :]
    c, s = _cis(k1 * j1, n1)
    fa = jnp.stack([c, -s], axis=1).reshape(2 * n1, n1h)
    fb = jnp.stack([s, c], axis=1).reshape(2 * n1, n1h)
    f1 = jnp.concatenate([fa, fb], axis=1)
    j2 = jnp.arange(n2, dtype=jnp.int32)[None, :]
    ca, sa = _cis(jnp.arange(n1, dtype=jnp.int32)[:, None] * j2, n)
    cb, sb = _cis(jnp.arange(n2, dtype=jnp.int32)[:, None] * j2, n2)
    ca, sa, cb, sb = ca[:, None, :], sa[:, None, :], cb[None], sb[None]
    c, s = ca * cb - sa * sb, sa * cb + ca * sb
    m2 = jnp.concatenate([jnp.concatenate([c, s], axis=2), jnp.concatenate([-s, c], axis=2)], axis=1)
    ct, st = jnp.swapaxes(c, 1, 2), jnp.swapaxes(s, 1, 2)
    m2i = jnp.concatenate([jnp.concatenate([ct, -st], axis=2), jnp.concatenate([st, ct], axis=2)], axis=1)
    j1 = jnp.arange(n1h, dtype=jnp.int32)[:, None]
    k1 = jnp.arange(n1, dtype=jnp.int32)[None, :]
    c, s = _cis(j1 * k1, n1)
    c, s = c / n, s / n
    g_re = jnp.stack([c, -s], axis=2).reshape(n1h, 2 * n1)
    g_im = jnp.stack([s, c], axis=2).reshape(n1h, 2 * n1)
    g1 = jnp.concatenate([g_re, g_im], axis=0)
    return f1.astype(BF16), m2.astype(BF16), m2i.astype(BF16), g1.astype(BF16)


FFT_UNROLL = 8


def _re_im_rows(k1, n2):
    pitch = n2 + ROW_PAD
    return (pl.ds(pl.multiple_of(k1 * 2 * pitch, 8), n2), pl.ds(pl.multiple_of(k1 * 2 * pitch + pitch, 8), n2))


def _hyspec_kernel(fa_ref, m2_ref, bias_ref, hf_ref, hb_ref, h_ref, af_ref, ab_ref, *, n1, n2):
    n1h = n1 // 2
    pitch = n2 + ROW_PAD
    fa = fa_ref[...]

    def stage1(j2, carry):
        src = pl.ds(j2, n1h, stride=pitch)
        dst = pl.ds(j2, 2 * n1, stride=pitch)
        af_ref[dst, :] = jnp.dot(fa, hf_ref[0, 0, src, :].astype(BF16), preferred_element_type=F32)
        ab_ref[dst, :] = jnp.dot(fa, hb_ref[0, 0, src, :].astype(BF16), preferred_element_type=F32)
        return carry

    lax.fori_loop(0, n2, stage1, 0, unroll=min(FFT_UNROLL, n2))
    bias = bias_ref[...]

    def stage2(k1, carry):
        re, im = _re_im_rows(k1, n2)
        sf = jnp.concatenate([af_ref[re, :], af_ref[im, :]], axis=0).astype(BF16)
        sb = jnp.concatenate([ab_ref[re, :], ab_ref[im, :]], axis=0).astype(BF16)
        xf = jnp.dot(m2_ref[k1], sf, preferred_element_type=F32)
        xb = jnp.dot(m2_ref[k1], sb, preferred_element_type=F32)
        h_ref[k1, :n2, :] = (xf[:n2] + xb[:n2] + bias).astype(BF16)
        h_ref[k1, n2:, :] = (xf[n2:] - xb[n2:]).astype(BF16)
        return carry

    lax.fori_loop(0, n1, stage2, 0, unroll=min(FFT_UNROLL, n1))


def _hyconv_kernel(f1_ref, m2_ref, m2i_ref, g1_ref, h_ref, ga_ref, gb_ref, o_ref, a_ref, *, n1, n2):
    n1h = n1 // 2
    pitch = n2 + ROW_PAD
    f1 = f1_ref[...]
    pad = jnp.zeros((ROW_PAD, LANES), F32)
    for r in range(2):
        for gi in range(n1h):
            o_ref[r, 0, 0, gi * pitch + n2:(gi + 1) * pitch, :] = pad

    def stage1(j2, carry):
        src = pl.ds(j2, n1h, stride=pitch)
        z = jnp.concatenate([ga_ref[0, 0, src, :], gb_ref[0, 0, src, :]], axis=0).astype(BF16)
        a_ref[pl.ds(j2, 2 * n1, stride=pitch), :] = jnp.dot(f1, z, preferred_element_type=F32)
        return carry

    lax.fori_loop(0, n2, stage1, 0, unroll=min(FFT_UNROLL, n2))

    def stage2(k1, carry):
        re, im = _re_im_rows(k1, n2)
        a = jnp.concatenate([a_ref[re, :], a_ref[im, :]], axis=0).astype(BF16)
        x = jnp.dot(m2_ref[k1], a, preferred_element_type=F32)
        xr, xi = x[:n2], x[n2:]
        hr, hi = h_ref[k1, :n2, :].astype(F32), h_ref[k1, n2:, :].astype(F32)
        y = jnp.concatenate([xr * hr - xi * hi, xr * hi + xi * hr], axis=0).astype(BF16)
        b = jnp.dot(m2i_ref[k1], y, preferred_element_type=F32)
        a_ref[re, :] = b[:n2]
        a_ref[im, :] = b[n2:]
        return carry

    lax.fori_loop(0, n1, stage2, 0, unroll=min(FFT_UNROLL, n1))
    g1 = g1_ref[...]

    def stage3(j2, carry):
        bs = a_ref[pl.ds(j2, 2 * n1, stride=pitch), :].astype(BF16)
        y = jnp.dot(g1, bs, preferred_element_type=F32)
        dst = pl.ds(j2, n1h, stride=pitch)
        o_ref[0, 0, 0, dst, :] = y[:n1h]
        o_ref[1, 0, 0, dst, :] = y[n1h:]
        return carry

    lax.fori_loop(0, n2, stage3, 0, unroll=min(FFT_UNROLL, n2))


def _long_conv(g, hfb, filt_bias, seq):
    b, nct, seqp, _ = g.shape
    pairs = b // 2
    n1, n2 = _fft_split(seq)
    f1, m2, m2i, g1 = _dft_tables(n1, n2)
    fa = f1[:, :n1 // 2]
    rows = 2 * n1 * (n2 + ROW_PAD)
    const = lambda a, nd: pl.BlockSpec(a.shape, (lambda t: (0,) * a.ndim) if nd == 1 else (lambda t, p: (0,) * a.ndim),
                                       pipeline_mode=pl.Buffered(1))
    h = pl.pallas_call(
        functools.partial(_hyspec_kernel, n1=n1, n2=n2),
        grid=(nct,),
        in_specs=[const(fa, 1), const(m2, 1), pl.BlockSpec((1, LANES), lambda t: (0, t)),
                  pl.BlockSpec((1, 1, seqp, LANES), lambda t: (0, t, 0, 0)),
                  pl.BlockSpec((1, 1, seqp, LANES), lambda t: (1, t, 0, 0))],
        out_specs=pl.BlockSpec((n1, 2 * n2, LANES), lambda t: (0, 0, t)),
        out_shape=jax.ShapeDtypeStruct((n1, 2 * n2, nct * LANES), BF16),
        scratch_shapes=[pltpu.VMEM((rows, LANES), F32), pltpu.VMEM((rows, LANES), F32)],
        compiler_params=_cparams(("parallel",), 48),
        name="hyspectrum",
    )(fa, m2, filt_bias, hfb, hfb)
    seqblk = lambda off: pl.BlockSpec((1, 1, seqp, LANES), lambda t, p: (p + off, t, 0, 0))
    y = pl.pallas_call(
        functools.partial(_hyconv_kernel, n1=n1, n2=n2),
        grid=(nct, pairs),
        in_specs=[const(f1, 2), const(m2, 2), const(m2i, 2), const(g1, 2),
                  pl.BlockSpec((n1, 2 * n2, LANES), lambda t, p: (0, 0, t)), seqblk(0), seqblk(pairs)],
        out_specs=pl.BlockSpec((2, 1, 1, seqp, LANES), lambda t, p: (0, p, t, 0, 0)),
        out_shape=jax.ShapeDtypeStruct((2, pairs, nct, seqp, LANES), F32),
        scratch_shapes=[pltpu.VMEM((rows, LANES), F32)],
        compiler_params=_cparams(("parallel", "parallel"), 48),
        name="hyconv",
    )(f1, m2, m2i, g1, h, g, g)
    return y.reshape(b, nct, seqp, LANES)


def _lane_fold(x, op):
    acc = x[:, :LANES]
    for t in range(1, x.shape[1] // LANES):
        acc = op(acc, x[:, t * LANES:(t + 1) * LANES])
    return acc


def _row_fold(x, op):
    parts = [x[t * 8:(t + 1) * 8] for t in range(x.shape[0] // 8)]
    while len(parts) > 1:
        parts = [op(parts[t], parts[t + 1]) for t in range(0, len(parts) - 1, 2)] + parts[len(parts) & ~1:]
    return parts[0]


def _attn_kernel(lq1_ref, lk1_ref, lq2_ref, lk2_ref, sg_ref, q_ref, k_ref, v_ref, o_ref,
                 s_ref, p_ref, m_ref, l_ref, vt_ref, *, lam_init, qb, kc):
    heads, seq = k_ref.shape[1], k_ref.shape[2]
    nk, nb = seq // kc, seq // qb
    lam = (jnp.exp(jnp.sum(lq1_ref[...] * lk1_ref[...], axis=-1, keepdims=True))
           - jnp.exp(jnp.sum(lq2_ref[...] * lk2_ref[...], axis=-1, keepdims=True)) + lam_init)
    lane = lax.broadcasted_iota(jnp.int32, (qb, HEAD_WIDTH), 1)
    zero = jnp.zeros((qb, HEAD_WIDTH), BF16)
    for h in range(heads):
        for j in range(nk):
            keys = slice(j * kc, (j + 1) * kc)
            vt_ref[h, :, keys] = v_ref[0, h, keys, :].astype(F32).T.astype(BF16)

    def head_rows(f):
        h, r = f // nb, f % nb
        return h, pl.ds(pl.multiple_of(r * qb, qb), qb)

    def stage(i_values, i_exps, i_scores):
        if i_values is not None:
            hv, rows_v = head_rows(i_values)
            l1 = jnp.sum(l_ref[0], axis=0, keepdims=True)
            l2 = jnp.sum(l_ref[1], axis=0, keepdims=True)
            ratio = (lam * l1 / l2).astype(BF16)
            ot = jnp.zeros((HEAD_WIDTH, qb), F32)
        if i_exps is not None:
            colmax = [jnp.max(m_ref[c], axis=0, keepdims=True) for c in range(2)]
            lacc = [jnp.zeros((8, qb), F32) for _ in range(2)]
        if i_scores is not None:
            hs, rows_s = head_rows(i_scores)
            q = q_ref[0, hs, rows_s, :]
            qc = (jnp.where(lane < HEAD_DIM, q, zero), jnp.where(lane < HEAD_DIM, zero, q))
            macc = [jnp.full((8, qb), -jnp.inf, F32) for _ in range(2)]
        for j in range(nk):
            keys = slice(j * kc, (j + 1) * kc)
            tie = None
            if i_exps is not None and j > 0:
                both = jnp.max(jnp.maximum(lacc[0], lacc[1]), axis=0, keepdims=True)
                tie = jnp.where(_lane_fold(both, jnp.maximum) > 1e30, 1.0, 0.0).astype(BF16)
            if i_values is not None:
                wt = p_ref[0, j] - p_ref[1, j] * ratio
                ot = ot + jnp.dot(vt_ref[hv, :, keys], wt, preferred_element_type=F32)
            if i_exps is not None:
                for c in range(2):
                    e = jnp.exp2(s_ref[c, j] - colmax[c])
                    p_ref[c, j] = e.astype(BF16)
                    lacc[c] = lacc[c] + _row_fold(e, jnp.add)
            if i_scores is not None:
                for c in range(2):
                    qcj = qc[c] if tie is None else qc[c] + tie
                    s = lax.dot_general(k_ref[0, hs, keys, :], qcj, (((1,), (1,)), ((), ())),
                                        preferred_element_type=F32)
                    s_ref[c, j] = s
                    macc[c] = jnp.maximum(macc[c], _row_fold(s, jnp.maximum))
        for c in range(2):
            if i_exps is not None:
                l_ref[c] = lacc[c]
            if i_scores is not None:
                m_ref[c] = macc[c]
        if i_values is not None:
            o = (ot * (1.0 / l1)).T
            o = _rms(o, SUBLN_EPS) * sg_ref[...] * (1.0 - lam_init)
            o_ref[0, hv, rows_v, :] = o.astype(BF16)

    def steady(i, carry):
        stage(i - 1, i, i + 1)
        return carry

    nf = heads * nb
    stage(None, None, 0)
    stage(None, 0, 1)
    lax.fori_loop(1, nf - 1, steady, 0)
    stage(nf - 2, nf - 1, None)
    stage(nf - 1, None, None)


def _attention(q, k, v, lq1, lk1, lq2, lk2, subln_gain, lam_init, qb, kc):
    b, heads, s, _ = q.shape
    assert heads * (s // qb) >= 2
    vec = lambda a: pl.BlockSpec(a.shape, lambda bi: (0, 0))
    seqblk = pl.BlockSpec((1, heads, s, HEAD_WIDTH), lambda bi: (bi, 0, 0, 0))
    return pl.pallas_call(
        functools.partial(_attn_kernel, lam_init=lam_init, qb=qb, kc=kc),
        grid=(b,),
        in_specs=[vec(lq1), vec(lk1), vec(lq2), vec(lk2), vec(subln_gain), seqblk, seqblk, seqblk],
        out_specs=seqblk,
        out_shape=jax.ShapeDtypeStruct((b, heads, s, HEAD_WIDTH), BF16),
        scratch_shapes=[pltpu.VMEM((2, s // kc, kc, qb), F32), pltpu.VMEM((2, s // kc, kc, qb), BF16),
                        pltpu.VMEM((2, 8, qb), F32), pltpu.VMEM((2, 8, qb), F32),
                        pltpu.VMEM((heads, HEAD_WIDTH, s), BF16)],
        compiler_params=_cparams(("parallel",), 58),
        name="diffattn",
    )(lq1, lk1, lq2, lk2, subln_gain, q, k, v)


def _post_kernel(yc_ref, x0_ref, ya_ref, x_ref, wo_ref, gmix_ref, gpre_ref, wu_ref, wd_ref, gpost_ref, o_ref,
                 *, c, n2, fc):
    pitch = n2 + ROW_PAD
    groups = range(x0_ref.shape[1] // n2)
    conv = jnp.concatenate([jnp.concatenate([yc_ref[0, t, gi * pitch:gi * pitch + n2, :] for gi in groups], axis=0)
                            for t in range(c // LANES)], axis=1)
    yh = (conv * x0_ref[0].astype(F32)).astype(BF16)
    ya = jnp.concatenate([ya_ref[0, h] for h in range(ya_ref.shape[1])], axis=1)
    mix = (jnp.dot(yh, wo_ref[:c, :], preferred_element_type=F32)
           + jnp.dot(ya, wo_ref[c:, :], preferred_element_type=F32))
    x1 = x_ref[0] + _rms(mix, NORM_EPS) * gmix_ref[...]
    hb = (_rms(x1, NORM_EPS) * gpre_ref[...]).astype(BF16)
    acc = jnp.zeros(x1.shape, F32)
    for j in range(wu_ref.shape[1] // fc):
        up = jnp.dot(hb, wu_ref[:, j * fc:(j + 1) * fc], preferred_element_type=F32)
        act = jnp.square(jnp.maximum(up, 0.0)).astype(BF16)
        acc = acc + jnp.dot(act, wd_ref[j * fc:(j + 1) * fc, :], preferred_element_type=F32)
    o_ref[0] = x1 + _rms(acc, NORM_EPS) * gpost_ref[...]


def _post(yc, x0, ya, x, wo_bf, gmix, gpre, wu_bf, wd_bf, gpost, tm, fc):
    b, s, d = x.shape
    c = x0.shape[2]
    row = lambda n: pl.BlockSpec((1, tm, n), lambda bi, i: (bi, i, 0))
    vec = pl.BlockSpec((1, d), lambda bi, i: (0, 0))
    resident = lambda a: pl.BlockSpec(a.shape, lambda bi, i: (0, 0), pipeline_mode=pl.Buffered(1))
    return pl.pallas_call(
        functools.partial(_post_kernel, c=c, n2=_fft_split(s)[1], fc=fc),
        grid=(b, s // tm),
        in_specs=[pl.BlockSpec((1, c // LANES, _padded_rows(tm, s), LANES), lambda bi, i: (bi, 0, i, 0)),
                  row(c), pl.BlockSpec((1, ya.shape[1], tm, HEAD_WIDTH), lambda bi, i: (bi, 0, i, 0)), row(d),
                  resident(wo_bf), vec, vec, resident(wu_bf), resident(wd_bf), vec],
        out_specs=row(d),
        out_shape=jax.ShapeDtypeStruct((b, s, d), F32),
        compiler_params=_cparams(("parallel", "parallel"), 56),
        name="outproj_mlp",
    )(yc, x0, ya, x, wo_bf, gmix, gpre, wu_bf, wd_bf, gpost)


def _tile(n, want):
    return min(n, want)


def kernel(x, attn_pre_gain, attn_post_gain, w_in, conv_w, conv_b, filt_w1, filt_b1, filt_w2, filt_b2, filt_w3, filt_b3, filt_w4, filt_freq, filt_bias, lam_q1, lam_k1, lam_q2, lam_k2, subln_gain, w_out, mlp_pre_gain, mlp_post_gain, w_up, w_down):
    b, s, d = x.shape
    depth = w_in.shape[0]
    c = filt_bias.shape[1]
    hy3 = 3 * c
    aw = (w_in.shape[2] - hy3) // 3
    row = lambda a, l: a[l][None, :]
    for l in range(depth):
        u, q, k, v = _inproj(x, row(attn_pre_gain, l), w_in[l].astype(BF16), hy3, aw, _tile(s, 512))
        g, x0 = _hygate(u, conv_w[l], row(conv_b, l), _tile(s, 512))
        hfb = _filters(s, c, filt_w1[l], row(filt_b1, l), filt_w2[l], row(filt_b2, l), filt_w3[l],
                       row(filt_b3, l), filt_w4[l], row(filt_freq, l), _tile(s, 512))
        yc = _long_conv(g, hfb, row(filt_bias, l), s)
        lam_init = 0.8 - 0.6 * math.exp(-0.3 * l)
        y_at = _attention(q, k, v, row(lam_q1, l), row(lam_k1, l), row(lam_q2, l), row(lam_k2, l),
                          row(subln_gain, l), lam_init, _tile(s, 256), _tile(s, 512))
        x = _post(yc, x0, y_at, x, w_out[l].astype(BF16), row(attn_post_gain, l),
                  row(mlp_pre_gain, l), w_up[l].astype(BF16), w_down[l].astype(BF16), row(mlp_post_gain, l),
                  _tile(s, 512), _tile(w_up.shape[2], 1024))
    return x
```

```python
import functools
import math

import jax
import jax.numpy as jnp
from jax import lax
from jax.experimental import pallas as pl
from jax.experimental.pallas import tpu as pltpu

F32 = jnp.float32
BF16 = jnp.bfloat16

HEAD_DIM = 64
HEAD_WIDTH = 2 * HEAD_DIM
FILTER_EMB_DIM = 33
FILTER_TARGET = 1e-2
FILTER_FAST_DECAY_PCT = 0.3
FILTER_SLOW_DECAY_PCT = 1.5
ROPE_THETA = 10000.0
NORM_EPS = 1e-6
SUBLN_EPS = 1e-5

LANES = 128
BF16_SUBLANES = 16
MIB = 1024 * 1024


def _cparams(semantics, vmem_mib):
    return pltpu.CompilerParams(dimension_semantics=semantics, vmem_limit_bytes=vmem_mib * MIB)


def _rms(x, eps):
    return x * lax.rsqrt(jnp.mean(x * x, axis=-1, keepdims=True) + eps)


def _fft_split(seq):
    n1 = min(128, seq // 8)
    return n1, 2 * seq // n1


ROW_PAD = 8


def _padded_rows(rows, seq):
    n2 = _fft_split(seq)[1]
    return rows // n2 * (n2 + ROW_PAD)


def _store_groups(ref, lead, x, n2):
    pitch = n2 + ROW_PAD
    pad = jnp.zeros((ROW_PAD, x.shape[1]), x.dtype)
    for gi in range(x.shape[0] // n2):
        ref[lead + (slice(gi * pitch, gi * pitch + n2),)] = x[gi * n2:(gi + 1) * n2]
        ref[lead + (slice(gi * pitch + n2, (gi + 1) * pitch),)] = pad


def _inproj_kernel(x_ref, g_ref, w_ref, cos_ref, sa_ref, sb_ref, u_ref, q_ref, k_ref, v_ref, *, hy3, aw, scale):
    x = x_ref[0]
    hb = (_rms(x, NORM_EPS) * g_ref[...]).astype(BF16)
    u_ref[0] = jnp.dot(hb, w_ref[:, :hy3], preferred_element_type=F32).astype(BF16)
    cos, sa, sb = cos_ref[...], sa_ref[...], sb_ref[...]

    def rope(t):
        return t * cos + pltpu.roll(t, LANES - HEAD_DIM // 2, 1) * sa + pltpu.roll(t, HEAD_DIM // 2, 1) * sb

    q = jnp.dot(hb, w_ref[:, hy3:hy3 + aw], preferred_element_type=F32)
    k = jnp.dot(hb, w_ref[:, hy3 + aw:hy3 + 2 * aw], preferred_element_type=F32)
    v = jnp.dot(hb, w_ref[:, hy3 + 2 * aw:], preferred_element_type=F32)
    for h in range(aw // HEAD_WIDTH):
        sl = slice(h * HEAD_WIDTH, (h + 1) * HEAD_WIDTH)
        q_ref[0, h] = (rope(q[:, sl]) * scale).astype(BF16)
        k_ref[0, h] = rope(k[:, sl]).astype(BF16)
        v_ref[0, h] = v[:, sl].astype(BF16)


def _rope_tables(seq):
    inv = ROPE_THETA ** (-jnp.arange(0, HEAD_DIM, 2, dtype=F32) / HEAD_DIM)
    ang = jnp.arange(seq, dtype=F32)[:, None] * inv[None, :]
    ang = jnp.tile(ang, (1, LANES // (HEAD_DIM // 2)))
    first = (jnp.arange(LANES) % HEAD_DIM) < HEAD_DIM // 2
    sin = jnp.sin(ang)
    return jnp.cos(ang), jnp.where(first, -sin, 0.0), jnp.where(first, 0.0, sin)


def _inproj(x, gain, w_bf, hy3, aw, tm):
    b, seq, d = x.shape
    heads = aw // HEAD_WIDTH
    cos, sa, sb = _rope_tables(seq)
    tab = pl.BlockSpec((tm, LANES), lambda bi, i: (i, 0))
    row = lambda n: pl.BlockSpec((1, tm, n), lambda bi, i: (bi, i, 0))
    head_major = pl.BlockSpec((1, heads, tm, HEAD_WIDTH), lambda bi, i: (bi, 0, i, 0))
    return pl.pallas_call(
        functools.partial(_inproj_kernel, hy3=hy3, aw=aw, scale=HEAD_DIM ** -0.5 * math.log2(math.e)),
        grid=(b, seq // tm),
        in_specs=[row(d), pl.BlockSpec((1, d), lambda bi, i: (0, 0)),
                  pl.BlockSpec(w_bf.shape, lambda bi, i: (0, 0)), tab, tab, tab],
        out_specs=[row(hy3), head_major, head_major, head_major],
        out_shape=[jax.ShapeDtypeStruct((b, seq, hy3), BF16)]
        + [jax.ShapeDtypeStruct((b, heads, seq, HEAD_WIDTH), BF16)] * 3,
        compiler_params=_cparams(("parallel", "parallel"), 48),
        name="inproj",
    )(x, gain, w_bf, cos, sa, sb)


def _hygate_kernel(prev_ref, cur_ref, next_ref, w_ref, b_ref, g_ref, x0_ref, *, c, n2):
    i = pl.program_id(1)
    last = pl.num_programs(1) - 1
    u = cur_ref[0].astype(F32)
    tl = u.shape[0]
    prev_row = prev_ref[0, BF16_SUBLANES - 1:BF16_SUBLANES, :].astype(F32) * jnp.where(i > 0, 1.0, 0.0)
    next_row = next_ref[0, 0:1, :].astype(F32) * jnp.where(i < last, 1.0, 0.0)
    rows = lax.broadcasted_iota(jnp.int32, u.shape, 0)
    up = jnp.where(rows == 0, prev_row, pltpu.roll(u, 1, 0))
    dn = jnp.where(rows == tl - 1, next_row, pltpu.roll(u, tl - 1, 0))
    y = up * w_ref[0:1, :] + u * w_ref[1:2, :] + dn * w_ref[2:3, :] + b_ref[...]
    x0_ref[0] = y[:, :c].astype(BF16)
    g = y[:, 2 * c:] * y[:, c:2 * c]
    for t in range(c // LANES):
        _store_groups(g_ref, (0, t), g[:, t * LANES:(t + 1) * LANES], n2)


def _hygate(u, conv_w, conv_b, tl):
    b, s, c3 = u.shape
    c = c3 // 3
    nct = c // LANES
    nb = tl // BF16_SUBLANES
    nhalo = s // BF16_SUBLANES
    return pl.pallas_call(
        functools.partial(_hygate_kernel, c=c, n2=_fft_split(s)[1]),
        grid=(b, s // tl),
        in_specs=[
            pl.BlockSpec((1, BF16_SUBLANES, c3), lambda bi, i: (bi, jnp.maximum(i * nb - 1, 0), 0)),
            pl.BlockSpec((1, tl, c3), lambda bi, i: (bi, i, 0)),
            pl.BlockSpec((1, BF16_SUBLANES, c3), lambda bi, i: (bi, jnp.minimum((i + 1) * nb, nhalo - 1), 0)),
            pl.BlockSpec((3, c3), lambda bi, i: (0, 0)),
            pl.BlockSpec((1, c3), lambda bi, i: (0, 0)),
        ],
        out_specs=[pl.BlockSpec((1, nct, _padded_rows(tl, s), LANES), lambda bi, i: (bi, 0, i, 0)),
                   pl.BlockSpec((1, tl, c), lambda bi, i: (bi, i, 0))],
        out_shape=[jax.ShapeDtypeStruct((b, nct, _padded_rows(s, s), LANES), F32),
                   jax.ShapeDtypeStruct((b, s, c), BF16)],
        compiler_params=_cparams(("parallel", "parallel"), 48),
        name="hygate",
    )(u, u, u, conv_w, conv_b)


def _filter_kernel(z_ref, dec_ref, w1_ref, b1_ref, w2_ref, b2_ref, w3_ref, b3_ref, w4_ref, fr_ref, o_ref, *, c, n2):
    dot = functools.partial(jnp.dot, preferred_element_type=F32, precision=lax.Precision.HIGHEST)
    fr = fr_ref[...]
    h = jnp.sin(fr * (dot(z_ref[...], w1_ref[...]) + b1_ref[...]))
    h = jnp.sin(fr * (dot(h, w2_ref[...]) + b2_ref[...]))
    h = jnp.sin(fr * (dot(h, w3_ref[...]) + b3_ref[...]))
    h = dot(h, w4_ref[...])
    dec = dec_ref[...]
    for d in range(2):
        hd = h[:, d * c:(d + 1) * c] * dec
        for t in range(c // LANES):
            _store_groups(o_ref, (d, t), hd[:, t * LANES:(t + 1) * LANES], n2)


def _filter_inputs(seq, c):
    bands = (FILTER_EMB_DIM - 1) // 2
    t = jnp.linspace(0.0, 1.0, seq, dtype=F32)[:, None]
    w = (2.0 * math.pi / seq) * jnp.arange(seq, dtype=F32)[:, None]
    f = jnp.linspace(1e-4, bands - 1, bands, dtype=F32)[None, :]
    fw = f * w
    z = jnp.concatenate([t, jnp.cos(fw), -jnp.sin(fw)], axis=-1)
    max_decay = math.log(FILTER_TARGET) / FILTER_FAST_DECAY_PCT
    min_decay = math.log(FILTER_TARGET) / FILTER_SLOW_DECAY_PCT
    deltas = jnp.linspace(min_decay, max_decay, c, dtype=F32)[None, :]
    return z, jnp.exp(-t * jnp.abs(deltas))


def _filters(seq, c, w1, b1, w2, b2, w3, b3, w4, freq, ts):
    z, decay = _filter_inputs(seq, c)
    order = w1.shape[1]
    zp = jnp.pad(z, ((0, 0), (0, order - FILTER_EMB_DIM)))
    w1p = jnp.pad(w1, ((0, order - FILTER_EMB_DIM), (0, 0)))
    full = lambda a: pl.BlockSpec(a.shape, lambda i: (0,) * a.ndim)
    args = (zp, decay, w1p, b1, w2, b2, w3, b3, w4, freq)
    return pl.pallas_call(
        functools.partial(_filter_kernel, c=c, n2=_fft_split(seq)[1]),
        grid=(seq // ts,),
        in_specs=[pl.BlockSpec((ts, order), lambda i: (i, 0)), pl.BlockSpec((ts, c), lambda i: (i, 0))]
        + [full(a) for a in args[2:]],
        out_specs=pl.BlockSpec((2, c // LANES, _padded_rows(ts, seq), LANES), lambda i: (0, 0, i, 0)),
        out_shape=jax.ShapeDtypeStruct((2, c // LANES, _padded_rows(seq, seq), LANES), F32),
        compiler_params=_cparams(("parallel",), 48),
        name="hyfilter",
    )(*args)


def _cis(m, n):
    ang = (m % n).astype(F32) * (2.0 * math.pi / n)
    return jnp.cos(ang), jnp.sin(ang)


def _dft_tables(n1, n2):
    n = n1 * n2
    n1h = n1 // 2
    k1 = jnp.arange(n1, dtype=jnp.int32)[:, None]
    j1 = jnp.arange(n1h, dtype=jnp.int32)[None, :]
    c, s = _cis(k1 * j1, n1)
    fa = jnp.stack([c, -s], axis=1).reshape(2 * n1, n1h)
    fb = jnp.stack([s, c], axis=1).reshape(2 * n1, n1h)
    f1 = jnp.concatenate([fa, fb], axis=1)
    j2 = jnp.arange(n2, dtype=jnp.int32)[None, :]
    ca, sa = _cis(jnp.arange(n1, dtype=jnp.int32)[:, None] * j2, n)
    cb, sb = _cis(jnp.arange(n2, dtype=jnp.int32)[:, None] * j2, n2)
    ca, sa, cb, sb = ca[:, None, :], sa[:, None, :], cb[None], sb[None]
    c, s = ca * cb - sa * sb, sa * cb + ca * sb
    m2 = jnp.concatenate([jnp.concatenate([c, s], axis=2), jnp.concatenate([-s, c], axis=2)], axis=1)
    ct, st = jnp.swapaxes(c, 1, 2), jnp.swapaxes(s, 1, 2)
    m2i = jnp.concatenate([jnp.concatenate([ct, -st], axis=2), jnp.concatenate([st, ct], axis=2)], axis=1)
    j1 = jnp.arange(n1h, dtype=jnp.int32)[:, None]
    k1 = jnp.arange(n1, dtype=jnp.int32)[None, :]
    c, s = _cis(j1 * k1, n1)
    c, s = c / n, s / n
    g_re = jnp.stack([c, -s], axis=2).reshape(n1h, 2 * n1)
    g_im = jnp.stack([s, c], axis=2).reshape(n1h, 2 * n1)
    g1 = jnp.concatenate([g_re, g_im], axis=0)
    return f1.astype(BF16), m2.astype(BF16), m2i.astype(BF16), g1.astype(BF16)


FFT_UNROLL = 32


def _re_im_rows(k1, n2):
    pitch = n2 + ROW_PAD
    return (pl.ds(pl.multiple_of(k1 * 2 * pitch, 8), n2), pl.ds(pl.multiple_of(k1 * 2 * pitch + pitch, 8), n2))


def _hyspec_kernel(fa_ref, m2_ref, bias_ref, hf_ref, hb_ref, h_ref, af_ref, ab_ref, *, n1, n2):
    n1h = n1 // 2
    pitch = n2 + ROW_PAD
    fa = fa_ref[...]

    def stage1(j2, carry):
        src = pl.ds(j2, n1h, stride=pitch)
        dst = pl.ds(j2, 2 * n1, stride=pitch)
        af_ref[dst, :] = jnp.dot(fa, hf_ref[0, 0, src, :].astype(BF16), preferred_element_type=F32)
        ab_ref[dst, :] = jnp.dot(fa, hb_ref[0, 0, src, :].astype(BF16), preferred_element_type=F32)
        return carry

    lax.fori_loop(0, n2, stage1, 0, unroll=min(FFT_UNROLL, n2))
    bias = bias_ref[...]

    def stage2(k1, carry):
        re, im = _re_im_rows(k1, n2)
        sf = jnp.concatenate([af_ref[re, :], af_ref[im, :]], axis=0).astype(BF16)
        sb = jnp.concatenate([ab_ref[re, :], ab_ref[im, :]], axis=0).astype(BF16)
        xf = jnp.dot(m2_ref[k1], sf, preferred_element_type=F32)
        xb = jnp.dot(m2_ref[k1], sb, preferred_element_type=F32)
        h_ref[k1, :n2, :] = (xf[:n2] + xb[:n2] + bias).astype(BF16)
        h_ref[k1, n2:, :] = (xf[n2:] - xb[n2:]).astype(BF16)
        return carry

    lax.fori_loop(0, n1, stage2, 0, unroll=min(FFT_UNROLL, n1))


def _hyconv_kernel(f1_ref, m2_ref, m2i_ref, g1_ref, h_ref, ga_ref, gb_ref, o_ref, a_ref, *, n1, n2):
    n1h = n1 // 2
    pitch = n2 + ROW_PAD
    f1 = f1_ref[...]
    pad = jnp.zeros((ROW_PAD, LANES), F32)
    for r in range(2):
        for gi in range(n1h):
            o_ref[r, 0, 0, gi * pitch + n2:(gi + 1) * pitch, :] = pad

    def stage1(j2, carry):
        src = pl.ds(j2, n1h, stride=pitch)
        z = jnp.concatenate([ga_ref[0, 0, src, :], gb_ref[0, 0, src, :]], axis=0).astype(BF16)
        a_ref[pl.ds(j2, 2 * n1, stride=pitch), :] = jnp.dot(f1, z, preferred_element_type=F32)
        return carry

    lax.fori_loop(0, n2, stage1, 0, unroll=min(FFT_UNROLL, n2))

    def stage2(k1, carry):
        re, im = _re_im_rows(k1, n2)
        a = jnp.concatenate([a_ref[re, :], a_ref[im, :]], axis=0).astype(BF16)
        x = jnp.dot(m2_ref[k1], a, preferred_element_type=F32)
        xr, xi = x[:n2], x[n2:]
        hr, hi = h_ref[k1, :n2, :].astype(F32), h_ref[k1, n2:, :].astype(F32)
        y = jnp.concatenate([xr * hr - xi * hi, xr * hi + xi * hr], axis=0).astype(BF16)
        b = jnp.dot(m2i_ref[k1], y, preferred_element_type=F32)
        a_ref[re, :] = b[:n2]
        a_ref[im, :] = b[n2:]
        return carry

    lax.fori_loop(0, n1, stage2, 0, unroll=min(FFT_UNROLL, n1))
    g1 = g1_ref[...]

    def stage3(j2, carry):
        bs = a_ref[pl.ds(j2, 2 * n1, stride=pitch), :].astype(BF16)
        y = jnp.dot(g1, bs, preferred_element_type=F32)
        dst = pl.ds(j2, n1h, stride=pitch)
        o_ref[0, 0, 0, dst, :] = y[:n1h]
        o_ref[1, 0, 0, dst, :] = y[n1h:]
        return carry

    lax.fori_loop(0, n2, stage3, 0, unroll=min(FFT_UNROLL, n2))


def _long_conv(g, hfb, filt_bias, seq):
    b, nct, seqp, _ = g.shape
    pairs = b // 2
    n1, n2 = _fft_split(seq)
    f1, m2, m2i, g1 = _dft_tables(n1, n2)
    fa = f1[:, :n1 // 2]
    rows = 2 * n1 * (n2 + ROW_PAD)
    const = lambda a, nd: pl.BlockSpec(a.shape, (lambda t: (0,) * a.ndim) if nd == 1 else (lambda t, p: (0,) * a.ndim),
                                       pipeline_mode=pl.Buffered(1))
    h = pl.pallas_call(
        functools.partial(_hyspec_kernel, n1=n1, n2=n2),
        grid=(nct,),
        in_specs=[const(fa, 1), const(m2, 1), pl.BlockSpec((1, LANES), lambda t: (0, t)),
                  pl.BlockSpec((1, 1, seqp, LANES), lambda t: (0, t, 0, 0)),
                  pl.BlockSpec((1, 1, seqp, LANES), lambda t: (1, t, 0, 0))],
        out_specs=pl.BlockSpec((n1, 2 * n2, LANES), lambda t: (0, 0, t)),
        out_shape=jax.ShapeDtypeStruct((n1, 2 * n2, nct * LANES), BF16),
        scratch_shapes=[pltpu.VMEM((rows, LANES), F32), pltpu.VMEM((rows, LANES), F32)],
        compiler_params=_cparams(("parallel",), 48),
        name="hyspectrum",
    )(fa, m2, filt_bias, hfb, hfb)
    seqblk = lambda off: pl.BlockSpec((1, 1, seqp, LANES), lambda t, p: (p + off, t, 0, 0))
    y = pl.pallas_call(
        functools.partial(_hyconv_kernel, n1=n1, n2=n2),
        grid=(nct, pairs),
        in_specs=[const(f1, 2), const(m2, 2), const(m2i, 2), const(g1, 2),
                  pl.BlockSpec((n1, 2 * n2, LANES), lambda t, p: (0, 0, t)), seqblk(0), seqblk(pairs)],
        out_specs=pl.BlockSpec((2, 1, 1, seqp, LANES), lambda t, p: (0, p, t, 0, 0)),
        out_shape=jax.ShapeDtypeStruct((2, pairs, nct, seqp, LANES), F32),
        scratch_shapes=[pltpu.VMEM((rows, LANES), F32)],
        compiler_params=_cparams(("parallel", "parallel"), 48),
        name="hyconv",
    )(f1, m2, m2i, g1, h, g, g)
    return y.reshape(b, nct, seqp, LANES)


def _lane_fold(x, op):
    acc = x[:, :LANES]
    for t in range(1, x.shape[1] // LANES):
        acc = op(acc, x[:, t * LANES:(t + 1) * LANES])
    return acc


def _row_fold(x, op):
    parts = [x[t * 8:(t + 1) * 8] for t in range(x.shape[0] // 8)]
    while len(parts) > 1:
        parts = [op(parts[t], parts[t + 1]) for t in range(0, len(parts) - 1, 2)] + parts[len(parts) & ~1:]
    return parts[0]


def _attn_kernel(par_ref, q_ref, k_ref, v_ref, o_ref, s_ref, p_ref, m_ref, l_ref, vt_ref, *, lam_init, qb, kc):
    heads, seq = k_ref.shape[1], k_ref.shape[2]
    nk, nb = seq // kc, seq // qb
    lam = (jnp.exp(jnp.sum(par_ref[0:1, :] * par_ref[1:2, :], axis=-1, keepdims=True))
           - jnp.exp(jnp.sum(par_ref[2:3, :] * par_ref[3:4, :], axis=-1, keepdims=True)) + lam_init)
    subln_gain = par_ref[4:5, :]
    lane = lax.broadcasted_iota(jnp.int32, (qb, HEAD_WIDTH), 1)
    zero = jnp.zeros((qb, HEAD_WIDTH), BF16)
    for h in range(heads):
        for j in range(nk):
            keys = slice(j * kc, (j + 1) * kc)
            vt_ref[h, :, keys] = v_ref[0, h, keys, :].astype(F32).T.astype(BF16)

    def head_rows(f):
        h, r = f // nb, f % nb
        return h, pl.ds(pl.multiple_of(r * qb, qb), qb)

    def stage(i_values, i_exps, i_scores):
        if i_values is not None:
            hv, rows_v = head_rows(i_values)
            l1 = jnp.sum(l_ref[0], axis=0, keepdims=True)
            l2 = jnp.sum(l_ref[1], axis=0, keepdims=True)
            ratio = (lam * l1 / l2).astype(BF16)
            ot = jnp.zeros((HEAD_WIDTH, qb), F32)
        if i_exps is not None:
            colmax = [jnp.max(m_ref[c], axis=0, keepdims=True) for c in range(2)]
            lacc = [jnp.zeros((8, qb), F32) for _ in range(2)]
        if i_scores is not None:
            hs, rows_s = head_rows(i_scores)
            q = q_ref[0, hs, rows_s, :]
            qc = (jnp.where(lane < HEAD_DIM, q, zero), jnp.where(lane < HEAD_DIM, zero, q))
            macc = [jnp.full((8, qb), -jnp.inf, F32) for _ in range(2)]
        for j in range(nk):
            keys = slice(j * kc, (j + 1) * kc)
            tie = None
            if i_exps is not None and j > 0:
                both = jnp.max(jnp.maximum(lacc[0], lacc[1]), axis=0, keepdims=True)
                tie = jnp.where(_lane_fold(both, jnp.maximum) > 1e30, 1.0, 0.0).astype(BF16)
            if i_values is not None:
                wt = p_ref[0, j] - p_ref[1, j] * ratio
                ot = ot + jnp.dot(vt_ref[hv, :, keys], wt, preferred_element_type=F32)
            if i_exps is not None:
                for c in range(2):
                    e = jnp.exp2(s_ref[c, j] - colmax[c])
                    p_ref[c, j] = e.astype(BF16)
                    lacc[c] = lacc[c] + _row_fold(e, jnp.add)
            if i_scores is not None:
                for c in range(2):
                    qcj = qc[c] if tie is None else qc[c] + tie
                    s = lax.dot_general(k_ref[0, hs, keys, :], qcj, (((1,), (1,)), ((), ())),
                                        preferred_element_type=F32)
                    s_ref[c, j] = s
                    macc[c] = jnp.maximum(macc[c], _row_fold(s, jnp.maximum))
        for c in range(2):
            if i_exps is not None:
                l_ref[c] = lacc[c]
            if i_scores is not None:
                m_ref[c] = macc[c]
        if i_values is not None:
            o = (ot * (1.0 / l1)).T
            o = _rms(o, SUBLN_EPS) * subln_gain * (1.0 - lam_init)
            o_ref[0, hv, rows_v, :] = o.astype(BF16)

    def steady(i, carry):
        stage(i - 1, i, i + 1)
        return carry

    nf = heads * nb
    stage(None, None, 0)
    stage(None, 0, 1)
    lax.fori_loop(1, nf - 1, steady, 0)
    stage(nf - 2, nf - 1, None)
    stage(nf - 1, None, None)


def _attention(q, k, v, lq1, lk1, lq2, lk2, subln_gain, lam_init, qb, kc):
    b, heads, s, _ = q.shape
    assert heads * (s // qb) >= 2
    pad = lambda a: jnp.pad(a, ((0, 0), (0, LANES - a.shape[1])))
    par = jnp.concatenate([pad(lq1), pad(lk1), pad(lq2), pad(lk2), subln_gain, jnp.zeros((11, LANES), F32)], axis=0)
    seqblk = pl.BlockSpec((1, heads, s, HEAD_WIDTH), lambda bi: (bi, 0, 0, 0))
    return pl.pallas_call(
        functools.partial(_attn_kernel, lam_init=lam_init, qb=qb, kc=kc),
        grid=(b,),
        in_specs=[pl.BlockSpec(par.shape, lambda bi: (0, 0)), seqblk, seqblk, seqblk],
        out_specs=seqblk,
        out_shape=jax.ShapeDtypeStruct((b, heads, s, HEAD_WIDTH), BF16),
        scratch_shapes=[pltpu.VMEM((2, s // kc, kc, qb), F32), pltpu.VMEM((2, s // kc, kc, qb), BF16),
                        pltpu.VMEM((2, 8, qb), F32), pltpu.VMEM((2, 8, qb), F32),
                        pltpu.VMEM((heads, HEAD_WIDTH, s), BF16)],
        compiler_params=_cparams(("parallel",), 58),
        name="diffattn",
    )(par, q, k, v)


def _post_kernel(yc_ref, x0_ref, ya_ref, x_ref, wo_ref, gmix_ref, gpre_ref, wu_ref, wd_ref, gpost_ref, o_ref,
                 *, c, n2, fc):
    pitch = n2 + ROW_PAD
    groups = range(x0_ref.shape[1] // n2)
    conv = jnp.concatenate([jnp.concatenate([yc_ref[0, t, gi * pitch:gi * pitch + n2, :] for gi in groups], axis=0)
                            for t in range(c // LANES)], axis=1)
    yh = (conv * x0_ref[0].astype(F32)).astype(BF16)
    ya = jnp.concatenate([ya_ref[0, h] for h in range(ya_ref.shape[1])], axis=1)
    mix = (jnp.dot(yh, wo_ref[:c, :], preferred_element_type=F32)
           + jnp.dot(ya, wo_ref[c:, :], preferred_element_type=F32))
    x1 = x_ref[0] + _rms(mix, NORM_EPS) * gmix_ref[...]
    hb = (_rms(x1, NORM_EPS) * gpre_ref[...]).astype(BF16)
    acc = jnp.zeros(x1.shape, F32)
    for j in range(wu_ref.shape[1] // fc):
        up = jnp.dot(hb, wu_ref[:, j * fc:(j + 1) * fc], preferred_element_type=F32)
        act = jnp.square(jnp.maximum(up, 0.0)).astype(BF16)
        acc = acc + jnp.dot(act, wd_ref[j * fc:(j + 1) * fc, :], preferred_element_type=F32)
    o_ref[0] = x1 + _rms(acc, NORM_EPS) * gpost_ref[...]


def _post(yc, x0, ya, x, wo_bf, gmix, gpre, wu_bf, wd_bf, gpost, tm, fc):
    b, s, d = x.shape
    c = x0.shape[2]
    row = lambda n: pl.BlockSpec((1, tm, n), lambda bi, i: (bi, i, 0))
    vec = pl.BlockSpec((1, d), lambda bi, i: (0, 0))
    resident = lambda a: pl.BlockSpec(a.shape, lambda bi, i: (0, 0), pipeline_mode=pl.Buffered(1))
    return pl.pallas_call(
        functools.partial(_post_kernel, c=c, n2=_fft_split(s)[1], fc=fc),
        grid=(b, s // tm),
        in_specs=[pl.BlockSpec((1, c // LANES, _padded_rows(tm, s), LANES), lambda bi, i: (bi, 0, i, 0)),
                  row(c), pl.BlockSpec((1, ya.shape[1], tm, HEAD_WIDTH), lambda bi, i: (bi, 0, i, 0)), row(d),
                  resident(wo_bf), vec, vec, resident(wu_bf), resident(wd_bf), vec],
        out_specs=row(d),
        out_shape=jax.ShapeDtypeStruct((b, s, d), F32),
        compiler_params=_cparams(("parallel", "parallel"), 56),
        name="outproj_mlp",
    )(yc, x0, ya, x, wo_bf, gmix, gpre, wu_bf, wd_bf, gpost)


def _tile(n, want):
    return min(n, want)


def kernel(x, attn_pre_gain, attn_post_gain, w_in, conv_w, conv_b, filt_w1, filt_b1, filt_w2, filt_b2, filt_w3, filt_b3, filt_w4, filt_freq, filt_bias, lam_q1, lam_k1, lam_q2, lam_k2, subln_gain, w_out, mlp_pre_gain, mlp_post_gain, w_up, w_down):
    b, s, d = x.shape
    depth = w_in.shape[0]
    c = filt_bias.shape[1]
    hy3 = 3 * c
    aw = (w_in.shape[2] - hy3) // 3
    row = lambda a, l: a[l][None, :]
    for l in range(depth):
        u, q, k, v = _inproj(x, row(attn_pre_gain, l), w_in[l].astype(BF16), hy3, aw, _tile(s, 512))
        g, x0 = _hygate(u, conv_w[l], row(conv_b, l), _tile(s, 512))
        hfb = _filters(s, c, filt_w1[l], row(filt_b1, l), filt_w2[l], row(filt_b2, l), filt_w3[l],
                       row(filt_b3, l), filt_w4[l], row(filt_freq, l), _tile(s, 512))
        yc = _long_conv(g, hfb, row(filt_bias, l), s)
        lam_init = 0.8 - 0.6 * math.exp(-0.3 * l)
        y_at = _attention(q, k, v, row(lam_q1, l), row(lam_k1, l), row(lam_q2, l), row(lam_k2, l),
                          row(subln_gain, l), lam_init, _tile(s, 256), _tile(s, 512))
        x = _post(yc, x0, y_at, x, w_out[l].astype(BF16), row(attn_post_gain, l),
                  row(mlp_pre_gain, l), w_up[l].astype(BF16), w_down[l].astype(BF16), row(mlp_post_gain, l),
                  _tile(s, 512), _tile(w_up.shape[2], 1024))
    return x
```

```python
import functools
import math

import jax
import jax.numpy as jnp
from jax import lax
from jax.experimental import pallas as pl
from jax.experimental.pallas import tpu as pltpu

F32 = jnp.float32
BF16 = jnp.bfloat16

HEAD_DIM = 64
HEAD_WIDTH = 2 * HEAD_DIM
FILTER_EMB_DIM = 33
FILTER_TARGET = 1e-2
FILTER_FAST_DECAY_PCT = 0.3
FILTER_SLOW_DECAY_PCT = 1.5
ROPE_THETA = 10000.0
NORM_EPS = 1e-6
SUBLN_EPS = 1e-5

LANES = 128
BF16_SUBLANES = 16
MIB = 1024 * 1024


def _cparams(semantics, vmem_mib):
    return pltpu.CompilerParams(dimension_semantics=semantics, vmem_limit_bytes=vmem_mib * MIB)


def _rms(x, eps):
    return x * lax.rsqrt(jnp.mean(x * x, axis=-1, keepdims=True) + eps)


def _fft_split(seq):
    n1 = min(128, seq // 8)
    return n1, 2 * seq // n1


ROW_PAD = 8


def _padded_rows(rows, seq):
    n2 = _fft_split(seq)[1]
    return rows // n2 * (n2 + ROW_PAD)


def _store_groups(ref, lead, x, n2):
    pitch = n2 + ROW_PAD
    pad = jnp.zeros((ROW_PAD, x.shape[1]), x.dtype)
    for gi in range(x.shape[0] // n2):
        ref[lead + (slice(gi * pitch, gi * pitch + n2),)] = x[gi * n2:(gi + 1) * n2]
        ref[lead + (slice(gi * pitch + n2, (gi + 1) * pitch),)] = pad


def _inproj_kernel(x_ref, g_ref, w_ref, wvt_ref, cos_ref, sa_ref, sb_ref, u_ref, q_ref, k_ref, vt_ref,
                   *, hy3, aw, scale):
    x = x_ref[0]
    hb = (_rms(x, NORM_EPS) * g_ref[...]).astype(BF16)
    u_ref[0] = jnp.dot(hb, w_ref[:, :hy3], preferred_element_type=F32).astype(BF16)
    cos, sa, sb = cos_ref[...], sa_ref[...], sb_ref[...]

    def rope(t):
        return t * cos + pltpu.roll(t, LANES - HEAD_DIM // 2, 1) * sa + pltpu.roll(t, HEAD_DIM // 2, 1) * sb

    q = jnp.dot(hb, w_ref[:, hy3:hy3 + aw], preferred_element_type=F32)
    k = jnp.dot(hb, w_ref[:, hy3 + aw:hy3 + 2 * aw], preferred_element_type=F32)
    vt = lax.dot_general(wvt_ref[...], hb, (((1,), (1,)), ((), ())), preferred_element_type=F32)
    for h in range(aw // HEAD_WIDTH):
        sl = slice(h * HEAD_WIDTH, (h + 1) * HEAD_WIDTH)
        q_ref[0, h] = (rope(q[:, sl]) * scale).astype(BF16)
        k_ref[0, h] = rope(k[:, sl]).astype(BF16)
        vt_ref[0, h] = vt[sl, :].astype(BF16)


def _rope_tables(seq):
    inv = ROPE_THETA ** (-jnp.arange(0, HEAD_DIM, 2, dtype=F32) / HEAD_DIM)
    ang = jnp.arange(seq, dtype=F32)[:, None] * inv[None, :]
    ang = jnp.tile(ang, (1, LANES // (HEAD_DIM // 2)))
    first = (jnp.arange(LANES) % HEAD_DIM) < HEAD_DIM // 2
    sin = jnp.sin(ang)
    return jnp.cos(ang), jnp.where(first, -sin, 0.0), jnp.where(first, 0.0, sin)


def _inproj(x, gain, w_bf, hy3, aw, tm):
    b, seq, d = x.shape
    heads = aw // HEAD_WIDTH
    cos, sa, sb = _rope_tables(seq)
    tab = pl.BlockSpec((tm, LANES), lambda bi, i: (i, 0))
    row = lambda n: pl.BlockSpec((1, tm, n), lambda bi, i: (bi, i, 0))
    head_major = pl.BlockSpec((1, heads, tm, HEAD_WIDTH), lambda bi, i: (bi, 0, i, 0))
    w_main, w_vt = w_bf[:, :hy3 + 2 * aw], w_bf[:, hy3 + 2 * aw:].T
    return pl.pallas_call(
        functools.partial(_inproj_kernel, hy3=hy3, aw=aw, scale=HEAD_DIM ** -0.5 * math.log2(math.e)),
        grid=(b, seq // tm),
        in_specs=[row(d), pl.BlockSpec((1, d), lambda bi, i: (0, 0)),
                  pl.BlockSpec(w_main.shape, lambda bi, i: (0, 0)), pl.BlockSpec(w_vt.shape, lambda bi, i: (0, 0)),
                  tab, tab, tab],
        out_specs=[row(hy3), head_major, head_major,
                   pl.BlockSpec((1, heads, HEAD_WIDTH, tm), lambda bi, i: (bi, 0, 0, i))],
        out_shape=[jax.ShapeDtypeStruct((b, seq, hy3), BF16)]
        + [jax.ShapeDtypeStruct((b, heads, seq, HEAD_WIDTH), BF16)] * 2
        + [jax.ShapeDtypeStruct((b, heads, HEAD_WIDTH, seq), BF16)],
        compiler_params=_cparams(("parallel", "parallel"), 48),
        name="inproj",
    )(x, gain, w_main, w_vt, cos, sa, sb)


def _hygate_kernel(prev_ref, cur_ref, next_ref, w_ref, b_ref, g_ref, x0_ref, *, c, n2):
    i = pl.program_id(1)
    last = pl.num_programs(1) - 1
    u = cur_ref[0].astype(F32)
    tl = u.shape[0]
    prev_row = prev_ref[0, BF16_SUBLANES - 1:BF16_SUBLANES, :].astype(F32) * jnp.where(i > 0, 1.0, 0.0)
    next_row = next_ref[0, 0:1, :].astype(F32) * jnp.where(i < last, 1.0, 0.0)
    rows = lax.broadcasted_iota(jnp.int32, u.shape, 0)
    up = jnp.where(rows == 0, prev_row, pltpu.roll(u, 1, 0))
    dn = jnp.where(rows == tl - 1, next_row, pltpu.roll(u, tl - 1, 0))
    y = up * w_ref[0:1, :] + u * w_ref[1:2, :] + dn * w_ref[2:3, :] + b_ref[...]
    x0_ref[0] = y[:, :c].astype(BF16)
    g = y[:, 2 * c:] * y[:, c:2 * c]
    for t in range(c // LANES):
        _store_groups(g_ref, (0, t), g[:, t * LANES:(t + 1) * LANES], n2)


def _hygate(u, conv_w, conv_b, tl):
    b, s, c3 = u.shape
    c = c3 // 3
    nct = c // LANES
    nb = tl // BF16_SUBLANES
    nhalo = s // BF16_SUBLANES
    return pl.pallas_call(
        functools.partial(_hygate_kernel, c=c, n2=_fft_split(s)[1]),
        grid=(b, s // tl),
        in_specs=[
            pl.BlockSpec((1, BF16_SUBLANES, c3), lambda bi, i: (bi, jnp.maximum(i * nb - 1, 0), 0)),
            pl.BlockSpec((1, tl, c3), lambda bi, i: (bi, i, 0)),
            pl.BlockSpec((1, BF16_SUBLANES, c3), lambda bi, i: (bi, jnp.minimum((i + 1) * nb, nhalo - 1), 0)),
            pl.BlockSpec((3, c3), lambda bi, i: (0, 0)),
            pl.BlockSpec((1, c3), lambda bi, i: (0, 0)),
        ],
        out_specs=[pl.BlockSpec((1, nct, _padded_rows(tl, s), LANES), lambda bi, i: (bi, 0, i, 0)),
                   pl.BlockSpec((1, tl, c), lambda bi, i: (bi, i, 0))],
        out_shape=[jax.ShapeDtypeStruct((b, nct, _padded_rows(s, s), LANES), F32),
                   jax.ShapeDtypeStruct((b, s, c), BF16)],
        compiler_params=_cparams(("parallel", "parallel"), 48),
        name="hygate",
    )(u, u, u, conv_w, conv_b)


def _filter_kernel(z_ref, dec_ref, w1_ref, b1_ref, w2_ref, b2_ref, w3_ref, b3_ref, w4_ref, fr_ref, o_ref, *, c, n2):
    dot = functools.partial(jnp.dot, preferred_element_type=F32, precision=lax.Precision.HIGHEST)
    fr = fr_ref[...]
    h = jnp.sin(fr * (dot(z_ref[...], w1_ref[...]) + b1_ref[...]))
    h = jnp.sin(fr * (dot(h, w2_ref[...]) + b2_ref[...]))
    h = jnp.sin(fr * (dot(h, w3_ref[...]) + b3_ref[...]))
    h = jnp.dot(h.astype(BF16), w4_ref[...].astype(BF16), preferred_element_type=F32)
    dec = dec_ref[...]
    for d in range(2):
        hd = h[:, d * c:(d + 1) * c] * dec
        for t in range(c // LANES):
            _store_groups(o_ref, (d, t), hd[:, t * LANES:(t + 1) * LANES], n2)


def _filter_inputs(seq, c):
    bands = (FILTER_EMB_DIM - 1) // 2
    t = jnp.linspace(0.0, 1.0, seq, dtype=F32)[:, None]
    w = (2.0 * math.pi / seq) * jnp.arange(seq, dtype=F32)[:, None]
    f = jnp.linspace(1e-4, bands - 1, bands, dtype=F32)[None, :]
    fw = f * w
    z = jnp.concatenate([t, jnp.cos(fw), -jnp.sin(fw)], axis=-1)
    max_decay = math.log(FILTER_TARGET) / FILTER_FAST_DECAY_PCT
    min_decay = math.log(FILTER_TARGET) / FILTER_SLOW_DECAY_PCT
    deltas = jnp.linspace(min_decay, max_decay, c, dtype=F32)[None, :]
    return z, jnp.exp(-t * jnp.abs(deltas))


def _filters(seq, c, w1, b1, w2, b2, w3, b3, w4, freq, ts):
    z, decay = _filter_inputs(seq, c)
    order = w1.shape[1]
    zp = jnp.pad(z, ((0, 0), (0, order - FILTER_EMB_DIM)))
    w1p = jnp.pad(w1, ((0, order - FILTER_EMB_DIM), (0, 0)))
    full = lambda a: pl.BlockSpec(a.shape, lambda i: (0,) * a.ndim)
    args = (zp, decay, w1p, b1, w2, b2, w3, b3, w4, freq)
    return pl.pallas_call(
        functools.partial(_filter_kernel, c=c, n2=_fft_split(seq)[1]),
        grid=(seq // ts,),
        in_specs=[pl.BlockSpec((ts, order), lambda i: (i, 0)), pl.BlockSpec((ts, c), lambda i: (i, 0))]
        + [full(a) for a in args[2:]],
        out_specs=pl.BlockSpec((2, c // LANES, _padded_rows(ts, seq), LANES), lambda i: (0, 0, i, 0)),
        out_shape=jax.ShapeDtypeStruct((2, c // LANES, _padded_rows(seq, seq), LANES), F32),
        compiler_params=_cparams(("parallel",), 48),
        name="hyfilter",
    )(*args)


def _cis(m, n):
    ang = (m % n).astype(F32) * (2.0 * math.pi / n)
    return jnp.cos(ang), jnp.sin(ang)


def _dft_tables(n1, n2):
    n = n1 * n2
    n1h = n1 // 2
    k1 = jnp.arange(n1, dtype=jnp.int32)[:, None]
    j1 = jnp.arange(n1h, dtype=jnp.int32)[None, :]
    c, s = _cis(k1 * j1, n1)
    fa = jnp.stack([c, -s], axis=1).reshape(2 * n1, n1h)
    fb = jnp.stack([s, c], axis=1).reshape(2 * n1, n1h)
    f1 = jnp.concatenate([fa, fb], axis=1)
    j2 = jnp.arange(n2, dtype=jnp.int32)[None, :]
    ca, sa = _cis(jnp.arange(n1, dtype=jnp.int32)[:, None] * j2, n)
    cb, sb = _cis(jnp.arange(n2, dtype=jnp.int32)[:, None] * j2, n2)
    ca, sa, cb, sb = ca[:, None, :], sa[:, None, :], cb[None], sb[None]
    c, s = ca * cb - sa * sb, sa * cb + ca * sb
    m2 = jnp.concatenate([jnp.concatenate([c, s], axis=2), jnp.concatenate([-s, c], axis=2)], axis=1)
    ct, st = jnp.swapaxes(c, 1, 2), jnp.swapaxes(s, 1, 2)
    m2i = jnp.concatenate([jnp.concatenate([ct, -st], axis=2), jnp.concatenate([st, ct], axis=2)], axis=1)
    j1 = jnp.arange(n1h, dtype=jnp.int32)[:, None]
    k1 = jnp.arange(n1, dtype=jnp.int32)[None, :]
    c, s = _cis(j1 * k1, n1)
    c, s = c / n, s / n
    g_re = jnp.stack([c, -s], axis=2).reshape(n1h, 2 * n1)
    g_im = jnp.stack([s, c], axis=2).reshape(n1h, 2 * n1)
    g1 = jnp.concatenate([g_re, g_im], axis=0)
    return f1.astype(BF16), m2.astype(BF16), m2i.astype(BF16), g1.astype(BF16)


FFT_UNROLL = 32


def _re_im_rows(k1, n2):
    pitch = n2 + ROW_PAD
    return (pl.ds(pl.multiple_of(k1 * 2 * pitch, 8), n2), pl.ds(pl.multiple_of(k1 * 2 * pitch + pitch, 8), n2))


def _hyspec_kernel(fa_ref, m2_ref, bias_ref, hf_ref, hb_ref, h_ref, af_ref, ab_ref, *, n1, n2):
    n1h = n1 // 2
    pitch = n2 + ROW_PAD
    fa = fa_ref[...]

    def stage1(j2, carry):
        src = pl.ds(j2, n1h, stride=pitch)
        dst = pl.ds(j2, 2 * n1, stride=pitch)
        af_ref[dst, :] = jnp.dot(fa, hf_ref[0, 0, src, :].astype(BF16), preferred_element_type=F32)
        ab_ref[dst, :] = jnp.dot(fa, hb_ref[0, 0, src, :].astype(BF16), preferred_element_type=F32)
        return carry

    lax.fori_loop(0, n2, stage1, 0, unroll=min(FFT_UNROLL, n2))
    bias = bias_ref[...]

    def stage2(k1, carry):
        re, im = _re_im_rows(k1, n2)
        sf = jnp.concatenate([af_ref[re, :], af_ref[im, :]], axis=0).astype(BF16)
        sb = jnp.concatenate([ab_ref[re, :], ab_ref[im, :]], axis=0).astype(BF16)
        xf = jnp.dot(m2_ref[k1], sf, preferred_element_type=F32)
        xb = jnp.dot(m2_ref[k1], sb, preferred_element_type=F32)
        h_ref[k1, :n2, :] = (xf[:n2] + xb[:n2] + bias).astype(BF16)
        h_ref[k1, n2:, :] = (xf[n2:] - xb[n2:]).astype(BF16)
        return carry

    lax.fori_loop(0, n1, stage2, 0, unroll=min(FFT_UNROLL, n1))


def _hyconv_kernel(f1_ref, m2_ref, m2i_ref, g1_ref, h_ref, ga_ref, gb_ref, o_ref, a_ref, *, n1, n2):
    n1h = n1 // 2
    pitch = n2 + ROW_PAD
    f1 = f1_ref[...]
    pad = jnp.zeros((ROW_PAD, LANES), F32)
    for r in range(2):
        for gi in range(n1h):
            o_ref[r, 0, 0, gi * pitch + n2:(gi + 1) * pitch, :] = pad

    def stage1(j2, carry):
        src = pl.ds(j2, n1h, stride=pitch)
        z = jnp.concatenate([ga_ref[0, 0, src, :], gb_ref[0, 0, src, :]], axis=0).astype(BF16)
        a_ref[pl.ds(j2, 2 * n1, stride=pitch), :] = jnp.dot(f1, z, preferred_element_type=F32)
        return carry

    lax.fori_loop(0, n2, stage1, 0, unroll=min(FFT_UNROLL, n2))

    def stage2(k1, carry):
        re, im = _re_im_rows(k1, n2)
        a = jnp.concatenate([a_ref[re, :], a_ref[im, :]], axis=0).astype(BF16)
        x = jnp.dot(m2_ref[k1], a, preferred_element_type=F32)
        xr, xi = x[:n2], x[n2:]
        hr, hi = h_ref[k1, :n2, :].astype(F32), h_ref[k1, n2:, :].astype(F32)
        y = jnp.concatenate([xr * hr - xi * hi, xr * hi + xi * hr], axis=0).astype(BF16)
        b = jnp.dot(m2i_ref[k1], y, preferred_element_type=F32)
        a_ref[re, :] = b[:n2]
        a_ref[im, :] = b[n2:]
        return carry

    lax.fori_loop(0, n1, stage2, 0, unroll=min(FFT_UNROLL, n1))
    g1 = g1_ref[...]

    def stage3(j2, carry):
        bs = a_ref[pl.ds(j2, 2 * n1, stride=pitch), :].astype(BF16)
        y = jnp.dot(g1, bs, preferred_element_type=F32)
        dst = pl.ds(j2, n1h, stride=pitch)
        o_ref[0, 0, 0, dst, :] = y[:n1h]
        o_ref[1, 0, 0, dst, :] = y[n1h:]
        return carry

    lax.fori_loop(0, n2, stage3, 0, unroll=min(FFT_UNROLL, n2))


def _long_conv(g, hfb, filt_bias, seq):
    b, nct, seqp, _ = g.shape
    pairs = b // 2
    n1, n2 = _fft_split(seq)
    f1, m2, m2i, g1 = _dft_tables(n1, n2)
    fa = f1[:, :n1 // 2]
    rows = 2 * n1 * (n2 + ROW_PAD)
    const = lambda a, nd: pl.BlockSpec(a.shape, (lambda t: (0,) * a.ndim) if nd == 1 else (lambda t, p: (0,) * a.ndim),
                                       pipeline_mode=pl.Buffered(1))
    h = pl.pallas_call(
        functools.partial(_hyspec_kernel, n1=n1, n2=n2),
        grid=(nct,),
        in_specs=[const(fa, 1), const(m2, 1), pl.BlockSpec((1, LANES), lambda t: (0, t)),
                  pl.BlockSpec((1, 1, seqp, LANES), lambda t: (0, t, 0, 0)),
                  pl.BlockSpec((1, 1, seqp, LANES), lambda t: (1, t, 0, 0))],
        out_specs=pl.BlockSpec((n1, 2 * n2, LANES), lambda t: (0, 0, t)),
        out_shape=jax.ShapeDtypeStruct((n1, 2 * n2, nct * LANES), BF16),
        scratch_shapes=[pltpu.VMEM((rows, LANES), F32), pltpu.VMEM((rows, LANES), F32)],
        compiler_params=_cparams(("parallel",), 48),
        name="hyspectrum",
    )(fa, m2, filt_bias, hfb, hfb)
    seqblk = lambda off: pl.BlockSpec((1, 1, seqp, LANES), lambda t, p: (p + off, t, 0, 0))
    y = pl.pallas_call(
        functools.partial(_hyconv_kernel, n1=n1, n2=n2),
        grid=(nct, pairs),
        in_specs=[const(f1, 2), const(m2, 2), const(m2i, 2), const(g1, 2),
                  pl.BlockSpec((n1, 2 * n2, LANES), lambda t, p: (0, 0, t)), seqblk(0), seqblk(pairs)],
        out_specs=pl.BlockSpec((2, 1, 1, seqp, LANES), lambda t, p: (0, p, t, 0, 0)),
        out_shape=jax.ShapeDtypeStruct((2, pairs, nct, seqp, LANES), F32),
        scratch_shapes=[pltpu.VMEM((rows, LANES), F32)],
        compiler_params=_cparams(("parallel", "parallel"), 48),
        name="hyconv",
    )(f1, m2, m2i, g1, h, g, g)
    return y.reshape(b, nct, seqp, LANES)


def _lane_fold(x, op):
    acc = x[:, :LANES]
    for t in range(1, x.shape[1] // LANES):
        acc = op(acc, x[:, t * LANES:(t + 1) * LANES])
    return acc


COMPONENT_SKEW_ROWS = 16


def _row_fold(x, op):
    parts = [x[t * 8:(t + 1) * 8] for t in range(x.shape[0] // 8)]
    while len(parts) > 1:
        parts = [op(parts[t], parts[t + 1]) for t in range(0, len(parts) - 1, 2)] + parts[len(parts) & ~1:]
    return parts[0]


def _attn_kernel(par_ref, q_ref, k_ref, vt_ref, o_ref, s_ref, p_ref, m_ref, l_ref, *, lam_init, qb, kc):
    heads, seq = k_ref.shape[1], k_ref.shape[2]
    nk, nb = seq // kc, seq // qb
    lam = (jnp.exp(jnp.sum(par_ref[0:1, :] * par_ref[1:2, :], axis=-1, keepdims=True))
           - jnp.exp(jnp.sum(par_ref[2:3, :] * par_ref[3:4, :], axis=-1, keepdims=True)) + lam_init)
    subln_gain = par_ref[4:5, :]
    lane = lax.broadcasted_iota(jnp.int32, (qb, HEAD_WIDTH), 1)
    zero = jnp.zeros((qb, HEAD_WIDTH), BF16)

    def head_rows(f):
        h, r = f // nb, f % nb
        return h, pl.ds(pl.multiple_of(r * qb, qb), qb)

    def stage(i_values, i_exps, i_scores):
        if i_values is not None:
            hv, rows_v = head_rows(i_values)
            l1 = jnp.sum(l_ref[0], axis=0, keepdims=True)
            l2 = jnp.sum(l_ref[1], axis=0, keepdims=True)
            ratio = (lam * l1 / l2).astype(BF16)
            ot = jnp.zeros((HEAD_WIDTH, qb), F32)
        if i_exps is not None:
            colmax = [jnp.max(m_ref[c], axis=0, keepdims=True) for c in range(2)]
            lacc = [jnp.zeros((8, qb), F32) for _ in range(2)]
        if i_scores is not None:
            hs, rows_s = head_rows(i_scores)
            q = q_ref[0, hs, rows_s, :]
            qc = (jnp.where(lane < HEAD_DIM, q, zero), jnp.where(lane < HEAD_DIM, zero, q))
            macc = [jnp.full((8, qb), -jnp.inf, F32) for _ in range(2)]
        for j in range(nk):
            keys = slice(j * kc, (j + 1) * kc)
            tie = None
            if i_exps is not None and j > 0:
                both = jnp.max(jnp.maximum(lacc[0], lacc[1]), axis=0, keepdims=True)
                tie = jnp.where(_lane_fold(both, jnp.maximum) > 1e30, 1.0, 0.0).astype(BF16)
            if i_values is not None:
                wt = p_ref[0, keys, :] - p_ref[1, keys, :] * ratio
                ot = ot + jnp.dot(vt_ref[0, hv, :, keys], wt, preferred_element_type=F32)
            if i_exps is not None:
                for c in range(2):
                    e = jnp.exp2(s_ref[c, keys, :] - colmax[c])
                    p_ref[c, keys, :] = e.astype(BF16)
                    lacc[c] = lacc[c] + _row_fold(e, jnp.add)
            if i_scores is not None:
                for c in range(2):
                    qcj = qc[c] if tie is None else qc[c] + tie
                    s = lax.dot_general(k_ref[0, hs, keys, :], qcj, (((1,), (1,)), ((), ())),
                                        preferred_element_type=F32)
                    s_ref[c, keys, :] = s
                    macc[c] = jnp.maximum(macc[c], _row_fold(s, jnp.maximum))
        for c in range(2):
            if i_exps is not None:
                l_ref[c] = lacc[c]
            if i_scores is not None:
                m_ref[c] = macc[c]
        if i_values is not None:
            o = (ot * (1.0 / l1)).T
            o = _rms(o, SUBLN_EPS) * subln_gain * (1.0 - lam_init)
            o_ref[0, hv, rows_v, :] = o.astype(BF16)

    def steady(i, carry):
        stage(i - 1, i, i + 1)
        return carry

    nf = heads * nb
    stage(None, None, 0)
    stage(None, 0, 1)
    lax.fori_loop(1, nf - 1, steady, 0)
    stage(nf - 2, nf - 1, None)
    stage(nf - 1, None, None)


def _attention(q, k, vt, lq1, lk1, lq2, lk2, subln_gain, lam_init, qb, kc):
    b, heads, s, _ = q.shape
    assert heads * (s // qb) >= 2
    pad = lambda a: jnp.pad(a, ((0, 0), (0, LANES - a.shape[1])))
    par = jnp.concatenate([pad(lq1), pad(lk1), pad(lq2), pad(lk2), subln_gain, jnp.zeros((11, LANES), F32)], axis=0)
    seqblk = pl.BlockSpec((1, heads, s, HEAD_WIDTH), lambda bi: (bi, 0, 0, 0))
    return pl.pallas_call(
        functools.partial(_attn_kernel, lam_init=lam_init, qb=qb, kc=kc),
        grid=(b,),
        in_specs=[pl.BlockSpec(par.shape, lambda bi: (0, 0)), seqblk, seqblk,
                  pl.BlockSpec((1, heads, HEAD_WIDTH, s), lambda bi: (bi, 0, 0, 0))],
        out_specs=seqblk,
        out_shape=jax.ShapeDtypeStruct((b, heads, s, HEAD_WIDTH), BF16),
        scratch_shapes=[pltpu.VMEM((2, s + COMPONENT_SKEW_ROWS, qb), F32),
                        pltpu.VMEM((2, s + COMPONENT_SKEW_ROWS, qb), BF16),
                        pltpu.VMEM((2, 8, qb), F32), pltpu.VMEM((2, 8, qb), F32)],
        compiler_params=_cparams(("parallel",), 56),
        name="diffattn",
    )(par, q, k, vt)


def _post_kernel(yc_ref, x0_ref, ya_ref, x_ref, wo_ref, gmix_ref, gpre_ref, wu_ref, wd_ref, gpost_ref, o_ref,
                 *, c, n2, fc):
    pitch = n2 + ROW_PAD
    groups = range(x0_ref.shape[1] // n2)
    conv = jnp.concatenate([jnp.concatenate([yc_ref[0, t, gi * pitch:gi * pitch + n2, :] for gi in groups], axis=0)
                            for t in range(c // LANES)], axis=1)
    yh = (conv * x0_ref[0].astype(F32)).astype(BF16)
    ya = jnp.concatenate([ya_ref[0, h] for h in range(ya_ref.shape[1])], axis=1)
    mix = (jnp.dot(yh, wo_ref[:c, :], preferred_element_type=F32)
           + jnp.dot(ya, wo_ref[c:, :], preferred_element_type=F32))
    x1 = x_ref[0] + _rms(mix, NORM_EPS) * gmix_ref[...]
    hb = (_rms(x1, NORM_EPS) * gpre_ref[...]).astype(BF16)
    acc = jnp.zeros(x1.shape, F32)
    for j in range(wu_ref.shape[1] // fc):
        up = jnp.dot(hb, wu_ref[:, j * fc:(j + 1) * fc], preferred_element_type=F32)
        act = jnp.square(jnp.maximum(up, 0.0)).astype(BF16)
        acc = acc + jnp.dot(act, wd_ref[j * fc:(j + 1) * fc, :], preferred_element_type=F32)
    o_ref[0] = x1 + _rms(acc, NORM_EPS) * gpost_ref[...]


def _post(yc, x0, ya, x, wo_bf, gmix, gpre, wu_bf, wd_bf, gpost, tm, fc):
    b, s, d = x.shape
    c = x0.shape[2]
    row = lambda n: pl.BlockSpec((1, tm, n), lambda bi, i: (bi, i, 0))
    vec = pl.BlockSpec((1, d), lambda bi, i: (0, 0))
    resident = lambda a: pl.BlockSpec(a.shape, lambda bi, i: (0, 0), pipeline_mode=pl.Buffered(1))
    return pl.pallas_call(
        functools.partial(_post_kernel, c=c, n2=_fft_split(s)[1], fc=fc),
        grid=(b, s // tm),
        in_specs=[pl.BlockSpec((1, c // LANES, _padded_rows(tm, s), LANES), lambda bi, i: (bi, 0, i, 0)),
                  row(c), pl.BlockSpec((1, ya.shape[1], tm, HEAD_WIDTH), lambda bi, i: (bi, 0, i, 0)), row(d),
                  resident(wo_bf), vec, vec, resident(wu_bf), resident(wd_bf), vec],
        out_specs=row(d),
        out_shape=jax.ShapeDtypeStruct((b, s, d), F32),
        compiler_params=_cparams(("parallel", "parallel"), 56),
        name="outproj_mlp",
    )(yc, x0, ya, x, wo_bf, gmix, gpre, wu_bf, wd_bf, gpost)


def _tile(n, want):
    return min(n, want)


def kernel(x, attn_pre_gain, attn_post_gain, w_in, conv_w, conv_b, filt_w1, filt_b1, filt_w2, filt_b2, filt_w3, filt_b3, filt_w4, filt_freq, filt_bias, lam_q1, lam_k1, lam_q2, lam_k2, subln_gain, w_out, mlp_pre_gain, mlp_post_gain, w_up, w_down):
    b, s, d = x.shape
    depth = w_in.shape[0]
    c = filt_bias.shape[1]
    hy3 = 3 * c
    aw = (w_in.shape[2] - hy3) // 3
    row = lambda a, l: a[l][None, :]
    for l in range(depth):
        u, q, k, v = _inproj(x, row(attn_pre_gain, l), w_in[l].astype(BF16), hy3, aw, _tile(s, 512))
        g, x0 = _hygate(u, conv_w[l], row(conv_b, l), _tile(s, 512))
        hfb = _filters(s, c, filt_w1[l], row(filt_b1, l), filt_w2[l], row(filt_b2, l), filt_w3[l],
                       row(filt_b3, l), filt_w4[l], row(filt_freq, l), _tile(s, 512))
        yc = _long_conv(g, hfb, row(filt_bias, l), s)
        lam_init = 0.8 - 0.6 * math.exp(-0.3 * l)
        y_at = _attention(q, k, v, row(lam_q1, l), row(lam_k1, l), row(lam_q2, l), row(lam_k2, l),
                          row(subln_gain, l), lam_init, _tile(s, 256), _tile(s, 512))
        x = _post(yc, x0, y_at, x, w_out[l].astype(BF16), row(attn_post_gain, l),
                  row(mlp_pre_gain, l), w_up[l].astype(BF16), w_down[l].astype(BF16), row(mlp_post_gain, l),
                  _tile(s, 512), _tile(w_up.shape[2], 1024))
    return x
```

```python
import functools
import math

import jax
import jax.numpy as jnp
from jax import lax
from jax.experimental import pallas as pl
from jax.experimental.pallas import tpu as pltpu

F32 = jnp.float32
BF16 = jnp.bfloat16

HEAD_DIM = 64
HEAD_WIDTH = 2 * HEAD_DIM
FILTER_EMB_DIM = 33
FILTER_TARGET = 1e-2
FILTER_FAST_DECAY_PCT = 0.3
FILTER_SLOW_DECAY_PCT = 1.5
ROPE_THETA = 10000.0
NORM_EPS = 1e-6
SUBLN_EPS = 1e-5

LANES = 128
BF16_SUBLANES = 16
MIB = 1024 * 1024


def _cparams(semantics, vmem_mib):
    return pltpu.CompilerParams(dimension_semantics=semantics, vmem_limit_bytes=vmem_mib * MIB)


def _rms(x, eps):
    return x * lax.rsqrt(jnp.mean(x * x, axis=-1, keepdims=True) + eps)


def _fft_split(seq):
    n1 = min(128, seq // 8)
    return n1, 2 * seq // n1


ROW_PAD = 8


def _padded_rows(rows, seq):
    n2 = _fft_split(seq)[1]
    return rows // n2 * (n2 + ROW_PAD)


def _store_groups(ref, lead, x, n2):
    pitch = n2 + ROW_PAD
    pad = jnp.zeros((ROW_PAD, x.shape[1]), x.dtype)
    for gi in range(x.shape[0] // n2):
        ref[lead + (slice(gi * pitch, gi * pitch + n2),)] = x[gi * n2:(gi + 1) * n2]
        ref[lead + (slice(gi * pitch + n2, (gi + 1) * pitch),)] = pad


HALO = 16


def _inproj_kernel(prev_ref, x_ref, next_ref, gain_ref, w_ref, cw_ref, cb_ref, cos_ref, sa_ref, sb_ref,
                   g_ref, x0_ref, q_ref, k_ref, v_ref, *, c, aw, scale, n2):
    i = pl.program_id(1)
    last = pl.num_programs(1) - 1
    hy3 = 3 * c
    tm = x_ref.shape[1]
    gain = gain_ref[...]
    normed = lambda x: (_rms(x, NORM_EPS) * gain).astype(BF16)
    hb = normed(x_ref[0])
    he = jnp.concatenate([normed(prev_ref[0]), hb, normed(next_ref[0])], axis=0)
    ue = jnp.dot(he, w_ref[:, :hy3], preferred_element_type=F32)
    zero = jnp.zeros((8, hy3), F32)
    ue = jnp.concatenate([ue[:HALO - 8], jnp.where(i > 0, ue[HALO - 8:HALO], zero), ue[HALO:HALO + tm],
                          jnp.where(i < last, ue[HALO + tm:HALO + tm + 8], zero), ue[HALO + tm + 8:]], axis=0)
    rows = ue.shape[0]
    up = pltpu.roll(ue, 1, 0)[HALO:HALO + tm]
    dn = pltpu.roll(ue, rows - 1, 0)[HALO:HALO + tm]
    y = up * cw_ref[0:1, :] + ue[HALO:HALO + tm] * cw_ref[1:2, :] + dn * cw_ref[2:3, :] + cb_ref[...]
    x0_ref[0] = y[:, :c].astype(BF16)
    g = y[:, 2 * c:] * y[:, c:2 * c]
    for t in range(c // LANES):
        _store_groups(g_ref, (0, t), g[:, t * LANES:(t + 1) * LANES], n2)
    cos, sa, sb = cos_ref[...], sa_ref[...], sb_ref[...]

    def rope(t):
        return t * cos + pltpu.roll(t, LANES - HEAD_DIM // 2, 1) * sa + pltpu.roll(t, HEAD_DIM // 2, 1) * sb

    q = jnp.dot(hb, w_ref[:, hy3:hy3 + aw], preferred_element_type=F32)
    k = jnp.dot(hb, w_ref[:, hy3 + aw:hy3 + 2 * aw], preferred_element_type=F32)
    v = jnp.dot(hb, w_ref[:, hy3 + 2 * aw:], preferred_element_type=F32)
    for h in range(aw // HEAD_WIDTH):
        sl = slice(h * HEAD_WIDTH, (h + 1) * HEAD_WIDTH)
        q_ref[0, h] = (rope(q[:, sl]) * scale).astype(BF16)
        k_ref[0, h] = rope(k[:, sl]).astype(BF16)
        v_ref[0, h] = v[:, sl].astype(BF16)


def _rope_tables(seq):
    inv = ROPE_THETA ** (-jnp.arange(0, HEAD_DIM, 2, dtype=F32) / HEAD_DIM)
    ang = jnp.arange(seq, dtype=F32)[:, None] * inv[None, :]
    ang = jnp.tile(ang, (1, LANES // (HEAD_DIM // 2)))
    first = (jnp.arange(LANES) % HEAD_DIM) < HEAD_DIM // 2
    sin = jnp.sin(ang)
    return jnp.cos(ang), jnp.where(first, -sin, 0.0), jnp.where(first, 0.0, sin)


def _inproj(x, gain, w_bf, conv_w, conv_b, c, aw, tm):
    b, seq, d = x.shape
    heads = aw // HEAD_WIDTH
    nct = c // LANES
    cos, sa, sb = _rope_tables(seq)
    nb, nhalo = tm // HALO, seq // HALO
    tab = pl.BlockSpec((tm, LANES), lambda bi, i: (i, 0))
    row = lambda n: pl.BlockSpec((1, tm, n), lambda bi, i: (bi, i, 0))
    full = lambda a: pl.BlockSpec(a.shape, lambda bi, i: (0, 0))
    head_major = pl.BlockSpec((1, heads, tm, HEAD_WIDTH), lambda bi, i: (bi, 0, i, 0))
    return pl.pallas_call(
        functools.partial(_inproj_kernel, c=c, aw=aw, scale=HEAD_DIM ** -0.5 * math.log2(math.e),
                          n2=_fft_split(seq)[1]),
        grid=(b, seq // tm),
        in_specs=[pl.BlockSpec((1, HALO, d), lambda bi, i: (bi, jnp.maximum(i * nb - 1, 0), 0)), row(d),
                  pl.BlockSpec((1, HALO, d), lambda bi, i: (bi, jnp.minimum((i + 1) * nb, nhalo - 1), 0)),
                  full(gain), full(w_bf), full(conv_w), full(conv_b), tab, tab, tab],
        out_specs=[pl.BlockSpec((1, nct, _padded_rows(tm, seq), LANES), lambda bi, i: (bi, 0, i, 0)), row(c),
                   head_major, head_major, head_major],
        out_shape=[jax.ShapeDtypeStruct((b, nct, _padded_rows(seq, seq), LANES), F32),
                   jax.ShapeDtypeStruct((b, seq, c), BF16)]
        + [jax.ShapeDtypeStruct((b, heads, seq, HEAD_WIDTH), BF16)] * 3,
        compiler_params=_cparams(("parallel", "parallel"), 56),
        name="inproj",
    )(x, x, x, gain, w_bf, conv_w, conv_b, cos, sa, sb)


def _filter_kernel(z_ref, dec_ref, w1_ref, b1_ref, w2_ref, b2_ref, w3_ref, b3_ref, w4_ref, fr_ref, o_ref, *, c, n2):
    dot = functools.partial(jnp.dot, preferred_element_type=F32, precision=lax.Precision.HIGHEST)
    fr = fr_ref[...]
    h = jnp.sin(fr * (dot(z_ref[...], w1_ref[...]) + b1_ref[...]))
    h = jnp.sin(fr * (dot(h, w2_ref[...]) + b2_ref[...]))
    h = jnp.sin(fr * (dot(h, w3_ref[...]) + b3_ref[...]))
    h = jnp.dot(h.astype(BF16), w4_ref[...].astype(BF16), preferred_element_type=F32)
    dec = dec_ref[...]
    for d in range(2):
        hd = h[:, d * c:(d + 1) * c] * dec
        for t in range(c // LANES):
            _store_groups(o_ref, (d, t), hd[:, t * LANES:(t + 1) * LANES], n2)


def _filter_inputs(seq, c):
    bands = (FILTER_EMB_DIM - 1) // 2
    t = jnp.linspace(0.0, 1.0, seq, dtype=F32)[:, None]
    w = (2.0 * math.pi / seq) * jnp.arange(seq, dtype=F32)[:, None]
    f = jnp.linspace(1e-4, bands - 1, bands, dtype=F32)[None, :]
    fw = f * w
    z = jnp.concatenate([t, jnp.cos(fw), -jnp.sin(fw)], axis=-1)
    max_decay = math.log(FILTER_TARGET) / FILTER_FAST_DECAY_PCT
    min_decay = math.log(FILTER_TARGET) / FILTER_SLOW_DECAY_PCT
    deltas = jnp.linspace(min_decay, max_decay, c, dtype=F32)[None, :]
    return z, jnp.exp(-t * jnp.abs(deltas))


def _filters(seq, c, w1, b1, w2, b2, w3, b3, w4, freq, ts):
    z, decay = _filter_inputs(seq, c)
    order = w1.shape[1]
    zp = jnp.pad(z, ((0, 0), (0, order - FILTER_EMB_DIM)))
    w1p = jnp.pad(w1, ((0, order - FILTER_EMB_DIM), (0, 0)))
    full = lambda a: pl.BlockSpec(a.shape, lambda i: (0,) * a.ndim)
    args = (zp, decay, w1p, b1, w2, b2, w3, b3, w4, freq)
    return pl.pallas_call(
        functools.partial(_filter_kernel, c=c, n2=_fft_split(seq)[1]),
        grid=(seq // ts,),
        in_specs=[pl.BlockSpec((ts, order), lambda i: (i, 0)), pl.BlockSpec((ts, c), lambda i: (i, 0))]
        + [full(a) for a in args[2:]],
        out_specs=pl.BlockSpec((2, c // LANES, _padded_rows(ts, seq), LANES), lambda i: (0, 0, i, 0)),
        out_shape=jax.ShapeDtypeStruct((2, c // LANES, _padded_rows(seq, seq), LANES), F32),
        compiler_params=_cparams(("parallel",), 48),
        name="hyfilter",
    )(*args)


def _cis(m, n):
    ang = (m % n).astype(F32) * (2.0 * math.pi / n)
    return jnp.cos(ang), jnp.sin(ang)


def _dft_tables(n1, n2):
    n = n1 * n2
    n1h = n1 // 2
    k1 = jnp.arange(n1, dtype=jnp.int32)[:, None]
    j1 = jnp.arange(n1h, dtype=jnp.int32)[None, :]
    c, s = _cis(k1 * j1, n1)
    fa = jnp.stack([c, -s], axis=1).reshape(2 * n1, n1h)
    fb = jnp.stack([s, c], axis=1).reshape(2 * n1, n1h)
    f1 = jnp.concatenate([fa, fb], axis=1)
    j2 = jnp.arange(n2, dtype=jnp.int32)[None, :]
    ca, sa = _cis(jnp.arange(n1, dtype=jnp.int32)[:, None] * j2, n)
    cb, sb = _cis(jnp.arange(n2, dtype=jnp.int32)[:, None] * j2, n2)
    ca, sa, cb, sb = ca[:, None, :], sa[:, None, :], cb[None], sb[None]
    c, s = ca * cb - sa * sb, sa * cb + ca * sb
    m2 = jnp.concatenate([jnp.concatenate([c, s], axis=2), jnp.concatenate([-s, c], axis=2)], axis=1)
    ct, st = jnp.swapaxes(c, 1, 2), jnp.swapaxes(s, 1, 2)
    m2i = jnp.concatenate([jnp.concatenate([ct, -st], axis=2), jnp.concatenate([st, ct], axis=2)], axis=1)
    j1 = jnp.arange(n1h, dtype=jnp.int32)[:, None]
    k1 = jnp.arange(n1, dtype=jnp.int32)[None, :]
    c, s = _cis(j1 * k1, n1)
    c, s = c / n, s / n
    g_re = jnp.stack([c, -s], axis=2).reshape(n1h, 2 * n1)
    g_im = jnp.stack([s, c], axis=2).reshape(n1h, 2 * n1)
    g1 = jnp.concatenate([g_re, g_im], axis=0)
    return f1.astype(BF16), m2.astype(BF16), m2i.astype(BF16), g1.astype(BF16)


FFT_UNROLL = 32


def _re_im_rows(k1, n2):
    pitch = n2 + ROW_PAD
    return (pl.ds(pl.multiple_of(k1 * 2 * pitch, 8), n2), pl.ds(pl.multiple_of(k1 * 2 * pitch + pitch, 8), n2))


def _hyspec_kernel(fa_ref, m2_ref, bias_ref, hf_ref, hb_ref, h_ref, af_ref, ab_ref, *, n1, n2):
    n1h = n1 // 2
    pitch = n2 + ROW_PAD
    fa = fa_ref[...]

    def stage1(j2, carry):
        src = pl.ds(j2, n1h, stride=pitch)
        dst = pl.ds(j2, 2 * n1, stride=pitch)
        af_ref[dst, :] = jnp.dot(fa, hf_ref[0, 0, src, :].astype(BF16), preferred_element_type=F32)
        ab_ref[dst, :] = jnp.dot(fa, hb_ref[0, 0, src, :].astype(BF16), preferred_element_type=F32)
        return carry

    lax.fori_loop(0, n2, stage1, 0, unroll=min(FFT_UNROLL, n2))
    bias = bias_ref[...]

    def stage2(k1, carry):
        re, im = _re_im_rows(k1, n2)
        sf = jnp.concatenate([af_ref[re, :], af_ref[im, :]], axis=0).astype(BF16)
        sb = jnp.concatenate([ab_ref[re, :], ab_ref[im, :]], axis=0).astype(BF16)
        xf = jnp.dot(m2_ref[k1], sf, preferred_element_type=F32)
        xb = jnp.dot(m2_ref[k1], sb, preferred_element_type=F32)
        h_ref[k1, :n2, :] = (xf[:n2] + xb[:n2] + bias).astype(BF16)
        h_ref[k1, n2:, :] = (xf[n2:] - xb[n2:]).astype(BF16)
        return carry

    lax.fori_loop(0, n1, stage2, 0, unroll=min(FFT_UNROLL, n1))


def _hyconv_kernel(f1_ref, m2_ref, m2i_ref, g1_ref, h_ref, ga_ref, gb_ref, o_ref, a_ref, *, n1, n2):
    n1h = n1 // 2
    pitch = n2 + ROW_PAD
    f1 = f1_ref[...]
    pad = jnp.zeros((ROW_PAD, LANES), F32)
    for r in range(2):
        for gi in range(n1h):
            o_ref[r, 0, 0, gi * pitch + n2:(gi + 1) * pitch, :] = pad

    def stage1(j2, carry):
        src = pl.ds(j2, n1h, stride=pitch)
        z = jnp.concatenate([ga_ref[0, 0, src, :], gb_ref[0, 0, src, :]], axis=0).astype(BF16)
        a_ref[pl.ds(j2, 2 * n1, stride=pitch), :] = jnp.dot(f1, z, preferred_element_type=F32)
        return carry

    lax.fori_loop(0, n2, stage1, 0, unroll=min(FFT_UNROLL, n2))

    def stage2(k1, carry):
        re, im = _re_im_rows(k1, n2)
        a = jnp.concatenate([a_ref[re, :], a_ref[im, :]], axis=0).astype(BF16)
        x = jnp.dot(m2_ref[k1], a, preferred_element_type=F32)
        xr, xi = x[:n2], x[n2:]
        hr, hi = h_ref[k1, :n2, :].astype(F32), h_ref[k1, n2:, :].astype(F32)
        y = jnp.concatenate([xr * hr - xi * hi, xr * hi + xi * hr], axis=0).astype(BF16)
        b = jnp.dot(m2i_ref[k1], y, preferred_element_type=F32)
        a_ref[re, :] = b[:n2]
        a_ref[im, :] = b[n2:]
        return carry

    lax.fori_loop(0, n1, stage2, 0, unroll=min(FFT_UNROLL, n1))
    g1 = g1_ref[...]

    def stage3(j2, carry):
        bs = a_ref[pl.ds(j2, 2 * n1, stride=pitch), :].astype(BF16)
        y = jnp.dot(g1, bs, preferred_element_type=F32)
        dst = pl.ds(j2, n1h, stride=pitch)
        o_ref[0, 0, 0, dst, :] = y[:n1h]
        o_ref[1, 0, 0, dst, :] = y[n1h:]
        return carry

    lax.fori_loop(0, n2, stage3, 0, unroll=min(FFT_UNROLL, n2))


def _long_conv(g, hfb, filt_bias, seq):
    b, nct, seqp, _ = g.shape
    pairs = b // 2
    n1, n2 = _fft_split(seq)
    f1, m2, m2i, g1 = _dft_tables(n1, n2)
    fa = f1[:, :n1 // 2]
    rows = 2 * n1 * (n2 + ROW_PAD)
    const = lambda a, nd: pl.BlockSpec(a.shape, (lambda t: (0,) * a.ndim) if nd == 1 else (lambda t, p: (0,) * a.ndim),
                                       pipeline_mode=pl.Buffered(1))
    h = pl.pallas_call(
        functools.partial(_hyspec_kernel, n1=n1, n2=n2),
        grid=(nct,),
        in_specs=[const(fa, 1), const(m2, 1), pl.BlockSpec((1, LANES), lambda t: (0, t)),
                  pl.BlockSpec((1, 1, seqp, LANES), lambda t: (0, t, 0, 0)),
                  pl.BlockSpec((1, 1, seqp, LANES), lambda t: (1, t, 0, 0))],
        out_specs=pl.BlockSpec((n1, 2 * n2, LANES), lambda t: (0, 0, t)),
        out_shape=jax.ShapeDtypeStruct((n1, 2 * n2, nct * LANES), BF16),
        scratch_shapes=[pltpu.VMEM((rows, LANES), F32), pltpu.VMEM((rows, LANES), F32)],
        compiler_params=_cparams(("parallel",), 48),
        name="hyspectrum",
    )(fa, m2, filt_bias, hfb, hfb)
    seqblk = lambda off: pl.BlockSpec((1, 1, seqp, LANES), lambda t, p: (p + off, t, 0, 0))
    y = pl.pallas_call(
        functools.partial(_hyconv_kernel, n1=n1, n2=n2),
        grid=(nct, pairs),
        in_specs=[const(f1, 2), const(m2, 2), const(m2i, 2), const(g1, 2),
                  pl.BlockSpec((n1, 2 * n2, LANES), lambda t, p: (0, 0, t)), seqblk(0), seqblk(pairs)],
        out_specs=pl.BlockSpec((2, 1, 1, seqp, LANES), lambda t, p: (0, p, t, 0, 0)),
        out_shape=jax.ShapeDtypeStruct((2, pairs, nct, seqp, LANES), F32),
        scratch_shapes=[pltpu.VMEM((rows, LANES), F32)],
        compiler_params=_cparams(("parallel", "parallel"), 48),
        name="hyconv",
    )(f1, m2, m2i, g1, h, g, g)
    return y.reshape(b, nct, seqp, LANES)


def _lane_fold(x, op):
    acc = x[:, :LANES]
    for t in range(1, x.shape[1] // LANES):
        acc = op(acc, x[:, t * LANES:(t + 1) * LANES])
    return acc


def _row_fold(x, op):
    parts = [x[t * 8:(t + 1) * 8] for t in range(x.shape[0] // 8)]
    while len(parts) > 1:
        parts = [op(parts[t], parts[t + 1]) for t in range(0, len(parts) - 1, 2)] + parts[len(parts) & ~1:]
    return parts[0]


def _attn_kernel(par_ref, q_ref, k_ref, v_ref, o_ref, s_ref, p_ref, m_ref, l_ref, vt_ref, *, lam_init, qb, kc):
    heads, seq = k_ref.shape[1], k_ref.shape[2]
    nk, nb = seq // kc, seq // qb
    lam = (jnp.exp(jnp.sum(par_ref[0:1, :] * par_ref[1:2, :], axis=-1, keepdims=True))
           - jnp.exp(jnp.sum(par_ref[2:3, :] * par_ref[3:4, :], axis=-1, keepdims=True)) + lam_init)
    subln_gain = par_ref[4:5, :]
    lane = lax.broadcasted_iota(jnp.int32, (qb, HEAD_WIDTH), 1)
    zero = jnp.zeros((qb, HEAD_WIDTH), BF16)
    for h in range(heads):
        for j in range(nk):
            keys = slice(j * kc, (j + 1) * kc)
            vt_ref[h, :, keys] = v_ref[0, h, keys, :].astype(F32).T.astype(BF16)

    def head_rows(f):
        h, r = f // nb, f % nb
        return h, pl.ds(pl.multiple_of(r * qb, qb), qb)

    def stage(i_values, i_exps, i_scores):
        if i_values is not None:
            hv, rows_v = head_rows(i_values)
            l1 = jnp.sum(l_ref[0], axis=0, keepdims=True)
            l2 = jnp.sum(l_ref[1], axis=0, keepdims=True)
            ratio = (lam * l1 / l2).astype(BF16)
            ot = jnp.zeros((HEAD_WIDTH, qb), F32)
        if i_exps is not None:
            colmax = [jnp.max(m_ref[c], axis=0, keepdims=True) for c in range(2)]
            lacc = [jnp.zeros((8, qb), F32) for _ in range(2)]
        if i_scores is not None:
            hs, rows_s = head_rows(i_scores)
            q = q_ref[0, hs, rows_s, :]
            qc = (jnp.where(lane < HEAD_DIM, q, zero), jnp.where(lane < HEAD_DIM, zero, q))
            macc = [jnp.full((8, qb), -jnp.inf, F32) for _ in range(2)]
        for j in range(nk):
            keys = slice(j * kc, (j + 1) * kc)
            tie = None
            if i_exps is not None and j > 0:
                both = jnp.max(jnp.maximum(lacc[0], lacc[1]), axis=0, keepdims=True)
                tie = jnp.where(_lane_fold(both, jnp.maximum) > 1e30, 1.0, 0.0).astype(BF16)
            if i_values is not None:
                wt = p_ref[0, j] - p_ref[1, j] * ratio
                ot = ot + jnp.dot(vt_ref[hv, :, keys], wt, preferred_element_type=F32)
            if i_exps is not None:
                for c in range(2):
                    e = jnp.exp2(s_ref[c, j] - colmax[c])
                    p_ref[c, j] = e.astype(BF16)
                    lacc[c] = lacc[c] + _row_fold(e, jnp.add)
            if i_scores is not None:
                for c in range(2):
                    qcj = qc[c] if tie is None else qc[c] + tie
                    s = lax.dot_general(k_ref[0, hs, keys, :], qcj, (((1,), (1,)), ((), ())),
                                        preferred_element_type=F32)
                    s_ref[c, j] = s
                    macc[c] = jnp.maximum(macc[c], _row_fold(s, jnp.maximum))
        for c in range(2):
            if i_exps is not None:
                l_ref[c] = lacc[c]
            if i_scores is not None:
                m_ref[c] = macc[c]
        if i_values is not None:
            o = (ot * (1.0 / l1)).T
            o = _rms(o, SUBLN_EPS) * subln_gain * (1.0 - lam_init)
            o_ref[0, hv, rows_v, :] = o.astype(BF16)

    def steady(i, carry):
        stage(i - 1, i, i + 1)
        return carry

    nf = heads * nb
    stage(None, None, 0)
    stage(None, 0, 1)
    lax.fori_loop(1, nf - 1, steady, 0)
    stage(nf - 2, nf - 1, None)
    stage(nf - 1, None, None)


def _attention(q, k, v, lq1, lk1, lq2, lk2, subln_gain, lam_init, qb, kc):
    b, heads, s, _ = q.shape
    assert heads * (s // qb) >= 2
    pad = lambda a: jnp.pad(a, ((0, 0), (0, LANES - a.shape[1])))
    par = jnp.concatenate([pad(lq1), pad(lk1), pad(lq2), pad(lk2), subln_gain, jnp.zeros((11, LANES), F32)], axis=0)
    seqblk = pl.BlockSpec((1, heads, s, HEAD_WIDTH), lambda bi: (bi, 0, 0, 0))
    return pl.pallas_call(
        functools.partial(_attn_kernel, lam_init=lam_init, qb=qb, kc=kc),
        grid=(b,),
        in_specs=[pl.BlockSpec(par.shape, lambda bi: (0, 0)), seqblk, seqblk, seqblk],
        out_specs=seqblk,
        out_shape=jax.ShapeDtypeStruct((b, heads, s, HEAD_WIDTH), BF16),
        scratch_shapes=[pltpu.VMEM((2, s // kc, kc, qb), F32), pltpu.VMEM((2, s // kc, kc, qb), BF16),
                        pltpu.VMEM((2, 8, qb), F32), pltpu.VMEM((2, 8, qb), F32),
                        pltpu.VMEM((heads, HEAD_WIDTH, s), BF16)],
        compiler_params=_cparams(("parallel",), 58),
        name="diffattn",
    )(par, q, k, v)


def _post_kernel(yc_ref, x0_ref, ya_ref, x_ref, wo_ref, gmix_ref, gpre_ref, wu_ref, wd_ref, gpost_ref, o_ref,
                 *, c, n2, fc):
    pitch = n2 + ROW_PAD
    groups = range(x0_ref.shape[1] // n2)
    conv = jnp.concatenate([jnp.concatenate([yc_ref[0, t, gi * pitch:gi * pitch + n2, :] for gi in groups], axis=0)
                            for t in range(c // LANES)], axis=1)
    yh = (conv * x0_ref[0].astype(F32)).astype(BF16)
    ya = jnp.concatenate([ya_ref[0, h] for h in range(ya_ref.shape[1])], axis=1)
    mix = (jnp.dot(yh, wo_ref[:c, :], preferred_element_type=F32)
           + jnp.dot(ya, wo_ref[c:, :], preferred_element_type=F32))
    x1 = x_ref[0] + _rms(mix, NORM_EPS) * gmix_ref[...]
    hb = (_rms(x1, NORM_EPS) * gpre_ref[...]).astype(BF16)
    acc = jnp.zeros(x1.shape, F32)
    for j in range(wu_ref.shape[1] // fc):
        up = jnp.dot(hb, wu_ref[:, j * fc:(j + 1) * fc], preferred_element_type=F32)
        act = jnp.square(jnp.maximum(up, 0.0)).astype(BF16)
        acc = acc + jnp.dot(act, wd_ref[j * fc:(j + 1) * fc, :], preferred_element_type=F32)
    o_ref[0] = x1 + _rms(acc, NORM_EPS) * gpost_ref[...]


def _post(yc, x0, ya, x, wo_bf, gmix, gpre, wu_bf, wd_bf, gpost, tm, fc):
    b, s, d = x.shape
    c = x0.shape[2]
    row = lambda n: pl.BlockSpec((1, tm, n), lambda bi, i: (bi, i, 0))
    vec = pl.BlockSpec((1, d), lambda bi, i: (0, 0))
    resident = lambda a: pl.BlockSpec(a.shape, lambda bi, i: (0, 0), pipeline_mode=pl.Buffered(1))
    return pl.pallas_call(
        functools.partial(_post_kernel, c=c, n2=_fft_split(s)[1], fc=fc),
        grid=(b, s // tm),
        in_specs=[pl.BlockSpec((1, c // LANES, _padded_rows(tm, s), LANES), lambda bi, i: (bi, 0, i, 0)),
                  row(c), pl.BlockSpec((1, ya.shape[1], tm, HEAD_WIDTH), lambda bi, i: (bi, 0, i, 0)), row(d),
                  resident(wo_bf), vec, vec, resident(wu_bf), resident(wd_bf), vec],
        out_specs=row(d),
        out_shape=jax.ShapeDtypeStruct((b, s, d), F32),
        compiler_params=_cparams(("parallel", "parallel"), 56),
        name="outproj_mlp",
    )(yc, x0, ya, x, wo_bf, gmix, gpre, wu_bf, wd_bf, gpost)


def _tile(n, want):
    return min(n, want)


def kernel(x, attn_pre_gain, attn_post_gain, w_in, conv_w, conv_b, filt_w1, filt_b1, filt_w2, filt_b2, filt_w3, filt_b3, filt_w4, filt_freq, filt_bias, lam_q1, lam_k1, lam_q2, lam_k2, subln_gain, w_out, mlp_pre_gain, mlp_post_gain, w_up, w_down):
    b, s, d = x.shape
    depth = w_in.shape[0]
    c = filt_bias.shape[1]
    hy3 = 3 * c
    aw = (w_in.shape[2] - hy3) // 3
    row = lambda a, l: a[l][None, :]
    for l in range(depth):
        g, x0, q, k, v = _inproj(x, row(attn_pre_gain, l), w_in[l].astype(BF16), conv_w[l], row(conv_b, l), c, aw,
                                 _tile(s, 512))
        hfb = _filters(s, c, filt_w1[l], row(filt_b1, l), filt_w2[l], row(filt_b2, l), filt_w3[l],
                       row(filt_b3, l), filt_w4[l], row(filt_freq, l), _tile(s, 512))
        yc = _long_conv(g, hfb, row(filt_bias, l), s)
        lam_init = 0.8 - 0.6 * math.exp(-0.3 * l)
        y_at = _attention(q, k, v, row(lam_q1, l), row(lam_k1, l), row(lam_q2, l), row(lam_k2, l),
                          row(subln_gain, l), lam_init, _tile(s, 256), _tile(s, 512))
        x = _post(yc, x0, y_at, x, w_out[l].astype(BF16), row(attn_post_gain, l),
                  row(mlp_pre_gain, l), w_up[l].astype(BF16), w_down[l].astype(BF16), row(mlp_post_gain, l),
                  _tile(s, 512), _tile(w_up.shape[2], 1024))
    return x
```

```python
import functools
import math

import jax
import jax.numpy as jnp
from jax import lax
from jax.experimental import pallas as pl
from jax.experimental.pallas import tpu as pltpu

F32 = jnp.float32
BF16 = jnp.bfloat16

HEAD_DIM = 64
HEAD_WIDTH = 2 * HEAD_DIM
FILTER_EMB_DIM = 33
FILTER_TARGET = 1e-2
FILTER_FAST_DECAY_PCT = 0.3
FILTER_SLOW_DECAY_PCT = 1.5
ROPE_THETA = 10000.0
NORM_EPS = 1e-6
SUBLN_EPS = 1e-5

LANES = 128
MIB = 1024 * 1024


def _cparams(semantics, vmem_mib):
    return pltpu.CompilerParams(dimension_semantics=semantics, vmem_limit_bytes=vmem_mib * MIB)


def _rms(x, eps):
    return x * lax.rsqrt(jnp.mean(x * x, axis=-1, keepdims=True) + eps)


def _fft_split(seq):
    n1 = min(128, seq // 8)
    return n1, 2 * seq // n1


ROW_PAD = 8


def _padded_rows(rows, seq):
    n2 = _fft_split(seq)[1]
    return rows // n2 * (n2 + ROW_PAD)


def _store_groups(ref, lead, x, n2):
    pitch = n2 + ROW_PAD
    pad = jnp.zeros((ROW_PAD, x.shape[1]), x.dtype)
    for gi in range(x.shape[0] // n2):
        ref[lead + (slice(gi * pitch, gi * pitch + n2),)] = x[gi * n2:(gi + 1) * n2]
        ref[lead + (slice(gi * pitch + n2, (gi + 1) * pitch),)] = pad


HALO = 16


def _inproj_kernel(prev_ref, x_ref, next_ref, gain_ref, w_ref, cw_ref, cb_ref, cos_ref, sa_ref, sb_ref,
                   g_ref, x0_ref, q_ref, k_ref, v_ref, *, c, aw, scale, n2):
    i = pl.program_id(1)
    last = pl.num_programs(1) - 1
    hy3 = 3 * c
    tm = x_ref.shape[1]
    gain = gain_ref[...]
    normed = lambda x: (_rms(x, NORM_EPS) * gain).astype(BF16)
    hb = normed(x_ref[0])
    he = jnp.concatenate([normed(prev_ref[0]), hb, normed(next_ref[0])], axis=0)
    ue = jnp.dot(he, w_ref[:, :hy3], preferred_element_type=F32)
    zero = jnp.zeros((8, hy3), F32)
    ue = jnp.concatenate([ue[:HALO - 8], jnp.where(i > 0, ue[HALO - 8:HALO], zero), ue[HALO:HALO + tm],
                          jnp.where(i < last, ue[HALO + tm:HALO + tm + 8], zero), ue[HALO + tm + 8:]], axis=0)
    rows = ue.shape[0]
    up = pltpu.roll(ue, 1, 0)[HALO:HALO + tm]
    dn = pltpu.roll(ue, rows - 1, 0)[HALO:HALO + tm]
    y = up * cw_ref[0:1, :] + ue[HALO:HALO + tm] * cw_ref[1:2, :] + dn * cw_ref[2:3, :] + cb_ref[...]
    x0_ref[0] = y[:, :c].astype(BF16)
    g = y[:, 2 * c:] * y[:, c:2 * c]
    for t in range(c // LANES):
        _store_groups(g_ref, (0, t), g[:, t * LANES:(t + 1) * LANES], n2)
    cos, sa, sb = cos_ref[...], sa_ref[...], sb_ref[...]

    def rope(t):
        return t * cos + pltpu.roll(t, LANES - HEAD_DIM // 2, 1) * sa + pltpu.roll(t, HEAD_DIM // 2, 1) * sb

    q = jnp.dot(hb, w_ref[:, hy3:hy3 + aw], preferred_element_type=F32)
    k = jnp.dot(hb, w_ref[:, hy3 + aw:hy3 + 2 * aw], preferred_element_type=F32)
    v = jnp.dot(hb, w_ref[:, hy3 + 2 * aw:], preferred_element_type=F32)
    for h in range(aw // HEAD_WIDTH):
        sl = slice(h * HEAD_WIDTH, (h + 1) * HEAD_WIDTH)
        q_ref[0, h] = (rope(q[:, sl]) * scale).astype(BF16)
        k_ref[0, h] = rope(k[:, sl]).astype(BF16)
        v_ref[0, h] = v[:, sl].astype(BF16)


def _rope_tables(seq):
    inv = ROPE_THETA ** (-jnp.arange(0, HEAD_DIM, 2, dtype=F32) / HEAD_DIM)
    ang = jnp.arange(seq, dtype=F32)[:, None] * inv[None, :]
    ang = jnp.tile(ang, (1, LANES // (HEAD_DIM // 2)))
    first = (jnp.arange(LANES) % HEAD_DIM) < HEAD_DIM // 2
    sin = jnp.sin(ang)
    return jnp.cos(ang), jnp.where(first, -sin, 0.0), jnp.where(first, 0.0, sin)


def _inproj(x, gain, w_bf, conv_w, conv_b, c, aw, tm):
    b, seq, d = x.shape
    heads = aw // HEAD_WIDTH
    nct = c // LANES
    cos, sa, sb = _rope_tables(seq)
    nb, nhalo = tm // HALO, seq // HALO
    tab = pl.BlockSpec((tm, LANES), lambda bi, i: (i, 0))
    row = lambda n: pl.BlockSpec((1, tm, n), lambda bi, i: (bi, i, 0))
    full = lambda a: pl.BlockSpec(a.shape, lambda bi, i: (0, 0))
    head_major = pl.BlockSpec((1, heads, tm, HEAD_WIDTH), lambda bi, i: (bi, 0, i, 0))
    return pl.pallas_call(
        functools.partial(_inproj_kernel, c=c, aw=aw, scale=HEAD_DIM ** -0.5 * math.log2(math.e),
                          n2=_fft_split(seq)[1]),
        grid=(b, seq // tm),
        in_specs=[pl.BlockSpec((1, HALO, d), lambda bi, i: (bi, jnp.maximum(i * nb - 1, 0), 0)), row(d),
                  pl.BlockSpec((1, HALO, d), lambda bi, i: (bi, jnp.minimum((i + 1) * nb, nhalo - 1), 0)),
                  full(gain), full(w_bf), full(conv_w), full(conv_b), tab, tab, tab],
        out_specs=[pl.BlockSpec((1, nct, _padded_rows(tm, seq), LANES), lambda bi, i: (bi, 0, i, 0)), row(c),
                   head_major, head_major, head_major],
        out_shape=[jax.ShapeDtypeStruct((b, nct, _padded_rows(seq, seq), LANES), F32),
                   jax.ShapeDtypeStruct((b, seq, c), BF16)]
        + [jax.ShapeDtypeStruct((b, heads, seq, HEAD_WIDTH), BF16)] * 3,
        compiler_params=_cparams(("parallel", "parallel"), 56),
        name="inproj",
    )(x, x, x, gain, w_bf, conv_w, conv_b, cos, sa, sb)


def _filter_kernel(z_ref, dec_ref, w1_ref, b1_ref, w2_ref, b2_ref, w3_ref, b3_ref, w4_ref, fr_ref, o_ref, *, c, n2):
    dot = functools.partial(jnp.dot, preferred_element_type=F32, precision=lax.Precision.HIGHEST)
    fr = fr_ref[...]
    h = jnp.sin(fr * (dot(z_ref[...], w1_ref[...]) + b1_ref[...]))
    h = jnp.sin(fr * (dot(h, w2_ref[...]) + b2_ref[...]))
    h = jnp.sin(fr * (dot(h, w3_ref[...]) + b3_ref[...]))
    h = jnp.dot(h.astype(BF16), w4_ref[...].astype(BF16), preferred_element_type=F32)
    dec = dec_ref[...]
    for d in range(2):
        hd = h[:, d * c:(d + 1) * c] * dec
        for t in range(c // LANES):
            _store_groups(o_ref, (d, t), hd[:, t * LANES:(t + 1) * LANES], n2)


def _filter_inputs(seq, c):
    bands = (FILTER_EMB_DIM - 1) // 2
    t = jnp.linspace(0.0, 1.0, seq, dtype=F32)[:, None]
    w = (2.0 * math.pi / seq) * jnp.arange(seq, dtype=F32)[:, None]
    f = jnp.linspace(1e-4, bands - 1, bands, dtype=F32)[None, :]
    fw = f * w
    z = jnp.concatenate([t, jnp.cos(fw), -jnp.sin(fw)], axis=-1)
    max_decay = math.log(FILTER_TARGET) / FILTER_FAST_DECAY_PCT
    min_decay = math.log(FILTER_TARGET) / FILTER_SLOW_DECAY_PCT
    deltas = jnp.linspace(min_decay, max_decay, c, dtype=F32)[None, :]
    return z, jnp.exp(-t * jnp.abs(deltas))


def _filters(seq, c, w1, b1, w2, b2, w3, b3, w4, freq, ts):
    z, decay = _filter_inputs(seq, c)
    order = w1.shape[1]
    zp = jnp.pad(z, ((0, 0), (0, order - FILTER_EMB_DIM)))
    w1p = jnp.pad(w1, ((0, order - FILTER_EMB_DIM), (0, 0)))
    full = lambda a: pl.BlockSpec(a.shape, lambda i: (0,) * a.ndim)
    args = (zp, decay, w1p, b1, w2, b2, w3, b3, w4, freq)
    return pl.pallas_call(
        functools.partial(_filter_kernel, c=c, n2=_fft_split(seq)[1]),
        grid=(seq // ts,),
        in_specs=[pl.BlockSpec((ts, order), lambda i: (i, 0)), pl.BlockSpec((ts, c), lambda i: (i, 0))]
        + [full(a) for a in args[2:]],
        out_specs=pl.BlockSpec((2, c // LANES, _padded_rows(ts, seq), LANES), lambda i: (0, 0, i, 0)),
        out_shape=jax.ShapeDtypeStruct((2, c // LANES, _padded_rows(seq, seq), LANES), F32),
        compiler_params=_cparams(("parallel",), 48),
        name="hyfilter",
    )(*args)


def _cis(m, n):
    ang = (m % n).astype(F32) * (2.0 * math.pi / n)
    return jnp.cos(ang), jnp.sin(ang)


def _dft_tables(n1, n2):
    n = n1 * n2
    n1h = n1 // 2
    k1 = jnp.arange(n1, dtype=jnp.int32)[:, None]
    j1 = jnp.arange(n1h, dtype=jnp.int32)[None, :]
    c, s = _cis(k1 * j1, n1)
    fa = jnp.stack([c, -s], axis=1).reshape(2 * n1, n1h)
    fb = jnp.stack([s, c], axis=1).reshape(2 * n1, n1h)
    f1 = jnp.concatenate([fa, fb], axis=1)
    j2 = jnp.arange(n2, dtype=jnp.int32)[None, :]
    ca, sa = _cis(jnp.arange(n1, dtype=jnp.int32)[:, None] * j2, n)
    cb, sb = _cis(jnp.arange(n2, dtype=jnp.int32)[:, None] * j2, n2)
    ca, sa, cb, sb = ca[:, None, :], sa[:, None, :], cb[None], sb[None]
    c, s = ca * cb - sa * sb, sa * cb + ca * sb
    m2 = jnp.concatenate([jnp.concatenate([c, s], axis=2), jnp.concatenate([-s, c], axis=2)], axis=1)
    ct, st = jnp.swapaxes(c, 1, 2), jnp.swapaxes(s, 1, 2)
    m2i = jnp.concatenate([jnp.concatenate([ct, -st], axis=2), jnp.concatenate([st, ct], axis=2)], axis=1)
    j1 = jnp.arange(n1h, dtype=jnp.int32)[:, None]
    k1 = jnp.arange(n1, dtype=jnp.int32)[None, :]
    c, s = _cis(j1 * k1, n1)
    c, s = c / n, s / n
    g_re = jnp.stack([c, -s], axis=2).reshape(n1h, 2 * n1)
    g_im = jnp.stack([s, c], axis=2).reshape(n1h, 2 * n1)
    g1 = jnp.concatenate([g_re, g_im], axis=0)
    return f1.astype(BF16), m2.astype(BF16), m2i.astype(BF16), g1.astype(BF16)


FFT_UNROLL = 64


def _re_im_rows(k1, n2):
    pitch = n2 + ROW_PAD
    return (pl.ds(pl.multiple_of(k1 * 2 * pitch, 8), n2), pl.ds(pl.multiple_of(k1 * 2 * pitch + pitch, 8), n2))


def _hyspec_kernel(fa_ref, m2_ref, bias_ref, hf_ref, hb_ref, h_ref, af_ref, ab_ref, *, n1, n2):
    n1h = n1 // 2
    pitch = n2 + ROW_PAD
    fa = fa_ref[...]

    def stage1(j2, carry):
        src = pl.ds(j2, n1h, stride=pitch)
        dst = pl.ds(j2, 2 * n1, stride=pitch)
        af_ref[dst, :] = jnp.dot(fa, hf_ref[0, 0, src, :].astype(BF16), preferred_element_type=F32)
        ab_ref[dst, :] = jnp.dot(fa, hb_ref[0, 0, src, :].astype(BF16), preferred_element_type=F32)
        return carry

    lax.fori_loop(0, n2, stage1, 0, unroll=min(FFT_UNROLL, n2))
    bias = bias_ref[...]

    def stage2(k1, carry):
        re, im = _re_im_rows(k1, n2)
        sf = jnp.concatenate([af_ref[re, :], af_ref[im, :]], axis=0).astype(BF16)
        sb = jnp.concatenate([ab_ref[re, :], ab_ref[im, :]], axis=0).astype(BF16)
        xf = jnp.dot(m2_ref[k1], sf, preferred_element_type=F32)
        xb = jnp.dot(m2_ref[k1], sb, preferred_element_type=F32)
        h_ref[k1, :n2, :] = (xf[:n2] + xb[:n2] + bias).astype(BF16)
        h_ref[k1, n2:, :] = (xf[n2:] - xb[n2:]).astype(BF16)
        return carry

    lax.fori_loop(0, n1, stage2, 0, unroll=min(FFT_UNROLL, n1))


def _hyconv_kernel(f1_ref, m2_ref, m2i_ref, g1_ref, h_ref, ga_ref, gb_ref, o_ref, a_ref, *, n1, n2):
    n1h = n1 // 2
    pitch = n2 + ROW_PAD
    f1 = f1_ref[...]
    pad = jnp.zeros((ROW_PAD, LANES), F32)
    for r in range(2):
        for gi in range(n1h):
            o_ref[r, 0, 0, gi * pitch + n2:(gi + 1) * pitch, :] = pad

    def stage1(j2, carry):
        src = pl.ds(j2, n1h, stride=pitch)
        z = jnp.concatenate([ga_ref[0, 0, src, :], gb_ref[0, 0, src, :]], axis=0).astype(BF16)
        a_ref[pl.ds(j2, 2 * n1, stride=pitch), :] = jnp.dot(f1, z, preferred_element_type=F32)
        return carry

    lax.fori_loop(0, n2, stage1, 0, unroll=min(FFT_UNROLL, n2))

    def stage2(k1, carry):
        re, im = _re_im_rows(k1, n2)
        a = jnp.concatenate([a_ref[re, :], a_ref[im, :]], axis=0).astype(BF16)
        x = jnp.dot(m2_ref[k1], a, preferred_element_type=F32)
        xr, xi = x[:n2], x[n2:]
        hr, hi = h_ref[k1, :n2, :].astype(F32), h_ref[k1, n2:, :].astype(F32)
        y = jnp.concatenate([xr * hr - xi * hi, xr * hi + xi * hr], axis=0).astype(BF16)
        b = jnp.dot(m2i_ref[k1], y, preferred_element_type=F32)
        a_ref[re, :] = b[:n2]
        a_ref[im, :] = b[n2:]
        return carry

    lax.fori_loop(0, n1, stage2, 0, unroll=min(FFT_UNROLL, n1))
    g1 = g1_ref[...]

    def stage3(j2, carry):
        bs = a_ref[pl.ds(j2, 2 * n1, stride=pitch), :].astype(BF16)
        y = jnp.dot(g1, bs, preferred_element_type=F32)
        dst = pl.ds(j2, n1h, stride=pitch)
        o_ref[0, 0, 0, dst, :] = y[:n1h]
        o_ref[1, 0, 0, dst, :] = y[n1h:]
        return carry

    lax.fori_loop(0, n2, stage3, 0, unroll=min(FFT_UNROLL, n2))


def _long_conv(g, hfb, filt_bias, seq):
    b, nct, seqp, _ = g.shape
    pairs = b // 2
    n1, n2 = _fft_split(seq)
    f1, m2, m2i, g1 = _dft_tables(n1, n2)
    fa = f1[:, :n1 // 2]
    rows = 2 * n1 * (n2 + ROW_PAD)
    const = lambda a, nd: pl.BlockSpec(a.shape, (lambda t: (0,) * a.ndim) if nd == 1 else (lambda t, p: (0,) * a.ndim),
                                       pipeline_mode=pl.Buffered(1))
    h = pl.pallas_call(
        functools.partial(_hyspec_kernel, n1=n1, n2=n2),
        grid=(nct,),
        in_specs=[const(fa, 1), const(m2, 1), pl.BlockSpec((1, LANES), lambda t: (0, t)),
                  pl.BlockSpec((1, 1, seqp, LANES), lambda t: (0, t, 0, 0)),
                  pl.BlockSpec((1, 1, seqp, LANES), lambda t: (1, t, 0, 0))],
        out_specs=pl.BlockSpec((n1, 2 * n2, LANES), lambda t: (0, 0, t)),
        out_shape=jax.ShapeDtypeStruct((n1, 2 * n2, nct * LANES), BF16),
        scratch_shapes=[pltpu.VMEM((rows, LANES), F32), pltpu.VMEM((rows, LANES), F32)],
        compiler_params=_cparams(("parallel",), 48),
        name="hyspectrum",
    )(fa, m2, filt_bias, hfb, hfb)
    seqblk = lambda off: pl.BlockSpec((1, 1, seqp, LANES), lambda t, p: (p + off, t, 0, 0))
    y = pl.pallas_call(
        functools.partial(_hyconv_kernel, n1=n1, n2=n2),
        grid=(nct, pairs),
        in_specs=[const(f1, 2), const(m2, 2), const(m2i, 2), const(g1, 2),
                  pl.BlockSpec((n1, 2 * n2, LANES), lambda t, p: (0, 0, t)), seqblk(0), seqblk(pairs)],
        out_specs=pl.BlockSpec((2, 1, 1, seqp, LANES), lambda t, p: (0, p, t, 0, 0)),
        out_shape=jax.ShapeDtypeStruct((2, pairs, nct, seqp, LANES), F32),
        scratch_shapes=[pltpu.VMEM((rows, LANES), F32)],
        compiler_params=_cparams(("parallel", "parallel"), 48),
        name="hyconv",
    )(f1, m2, m2i, g1, h, g, g)
    return y.reshape(b, nct, seqp, LANES)


def _lane_fold(x, op):
    acc = x[:, :LANES]
    for t in range(1, x.shape[1] // LANES):
        acc = op(acc, x[:, t * LANES:(t + 1) * LANES])
    return acc


def _row_fold(x, op):
    parts = [x[t * 8:(t + 1) * 8] for t in range(x.shape[0] // 8)]
    while len(parts) > 1:
        parts = [op(parts[t], parts[t + 1]) for t in range(0, len(parts) - 1, 2)] + parts[len(parts) & ~1:]
    return parts[0]


def _attn_kernel(par_ref, q_ref, k_ref, v_ref, o_ref, s_ref, p_ref, m_ref, l_ref, vt_ref, *, lam_init, qb, kc):
    heads, seq = k_ref.shape[1], k_ref.shape[2]
    nk, nb = seq // kc, seq // qb
    lam = (jnp.exp(jnp.sum(par_ref[0:1, :] * par_ref[1:2, :], axis=-1, keepdims=True))
           - jnp.exp(jnp.sum(par_ref[2:3, :] * par_ref[3:4, :], axis=-1, keepdims=True)) + lam_init)
    subln_gain = par_ref[4:5, :]
    lane = lax.broadcasted_iota(jnp.int32, (qb, HEAD_WIDTH), 1)
    zero = jnp.zeros((qb, HEAD_WIDTH), BF16)
    for h in range(heads):
        for j in range(nk):
            keys = slice(j * kc, (j + 1) * kc)
            vt_ref[h, :, keys] = v_ref[0, h, keys, :].astype(F32).T.astype(BF16)

    def head_rows(f):
        h, r = f // nb, f % nb
        return h, pl.ds(pl.multiple_of(r * qb, qb), qb)

    def stage(i_values, i_exps, i_scores):
        if i_values is not None:
            hv, rows_v = head_rows(i_values)
            l1 = jnp.sum(l_ref[0], axis=0, keepdims=True)
            l2 = jnp.sum(l_ref[1], axis=0, keepdims=True)
            ratio = (lam * l1 / l2).astype(BF16)
            ot = jnp.zeros((HEAD_WIDTH, qb), F32)
        if i_exps is not None:
            colmax = [jnp.max(m_ref[c], axis=0, keepdims=True) for c in range(2)]
            lacc = [jnp.zeros((8, qb), F32) for _ in range(2)]
        if i_scores is not None:
            hs, rows_s = head_rows(i_scores)
            q = q_ref[0, hs, rows_s, :]
            qc = (jnp.where(lane < HEAD_DIM, q, zero), jnp.where(lane < HEAD_DIM, zero, q))
            macc = [jnp.full((8, qb), -jnp.inf, F32) for _ in range(2)]
        for j in range(nk):
            keys = slice(j * kc, (j + 1) * kc)
            tie = None
            if i_exps is not None and j > 0:
                both = jnp.max(jnp.maximum(lacc[0], lacc[1]), axis=0, keepdims=True)
                tie = jnp.where(_lane_fold(both, jnp.maximum) > 1e30, 1.0, 0.0).astype(BF16)
            if i_values is not None:
                wt = p_ref[0, j] - p_ref[1, j] * ratio
                ot = ot + jnp.dot(vt_ref[hv, :, keys], wt, preferred_element_type=F32)
            if i_exps is not None:
                for c in range(2):
                    e = jnp.exp2(s_ref[c, j] - colmax[c])
                    p_ref[c, j] = e.astype(BF16)
                    lacc[c] = lacc[c] + _row_fold(e, jnp.add)
            if i_scores is not None:
                for c in range(2):
                    qcj = qc[c] if tie is None else qc[c] + tie
                    s = lax.dot_general(k_ref[0, hs, keys, :], qcj, (((1,), (1,)), ((), ())),
                                        preferred_element_type=F32)
                    s_ref[c, j] = s
                    macc[c] = jnp.maximum(macc[c], _row_fold(s, jnp.maximum))
        for c in range(2):
            if i_exps is not None:
                l_ref[c] = lacc[c]
            if i_scores is not None:
                m_ref[c] = macc[c]
        if i_values is not None:
            o = (ot * (1.0 / l1)).T
            o = _rms(o, SUBLN_EPS) * subln_gain * (1.0 - lam_init)
            o_ref[0, hv, rows_v, :] = o.astype(BF16)

    def steady(i, carry):
        stage(i - 1, i, i + 1)
        return carry

    nf = heads * nb
    stage(None, None, 0)
    stage(None, 0, 1)
    lax.fori_loop(1, nf - 1, steady, 0)
    stage(nf - 2, nf - 1, None)
    stage(nf - 1, None, None)


def _attention(q, k, v, lq1, lk1, lq2, lk2, subln_gain, lam_init, qb, kc):
    b, heads, s, _ = q.shape
    assert heads * (s // qb) >= 2
    pad = lambda a: jnp.pad(a, ((0, 0), (0, LANES - a.shape[1])))
    par = jnp.concatenate([pad(lq1), pad(lk1), pad(lq2), pad(lk2), subln_gain, jnp.zeros((11, LANES), F32)], axis=0)
    seqblk = pl.BlockSpec((1, heads, s, HEAD_WIDTH), lambda bi: (bi, 0, 0, 0))
    return pl.pallas_call(
        functools.partial(_attn_kernel, lam_init=lam_init, qb=qb, kc=kc),
        grid=(b,),
        in_specs=[pl.BlockSpec(par.shape, lambda bi: (0, 0)), seqblk, seqblk, seqblk],
        out_specs=seqblk,
        out_shape=jax.ShapeDtypeStruct((b, heads, s, HEAD_WIDTH), BF16),
        scratch_shapes=[pltpu.VMEM((2, s // kc, kc, qb), F32), pltpu.VMEM((2, s // kc, kc, qb), BF16),
                        pltpu.VMEM((2, 8, qb), F32), pltpu.VMEM((2, 8, qb), F32),
                        pltpu.VMEM((heads, HEAD_WIDTH, s), BF16)],
        compiler_params=_cparams(("parallel",), 58),
        name="diffattn",
    )(par, q, k, v)


def _post_kernel(yc_ref, x0_ref, ya_ref, x_ref, wo_ref, gmix_ref, gpre_ref, wu_ref, wd_ref, gpost_ref, o_ref,
                 *, c, n2, fc):
    pitch = n2 + ROW_PAD
    groups = range(x0_ref.shape[1] // n2)
    conv = jnp.concatenate([jnp.concatenate([yc_ref[0, t, gi * pitch:gi * pitch + n2, :] for gi in groups], axis=0)
                            for t in range(c // LANES)], axis=1)
    yh = (conv * x0_ref[0].astype(F32)).astype(BF16)
    ya = jnp.concatenate([ya_ref[0, h] for h in range(ya_ref.shape[1])], axis=1)
    mix = (jnp.dot(yh, wo_ref[:c, :], preferred_element_type=F32)
           + jnp.dot(ya, wo_ref[c:, :], preferred_element_type=F32))
    x1 = x_ref[0] + _rms(mix, NORM_EPS) * gmix_ref[...]
    hb = (_rms(x1, NORM_EPS) * gpre_ref[...]).astype(BF16)
    acc = jnp.zeros(x1.shape, F32)
    for j in range(wu_ref.shape[1] // fc):
        up = jnp.dot(hb, wu_ref[:, j * fc:(j + 1) * fc], preferred_element_type=F32)
        act = jnp.square(jnp.maximum(up, 0.0)).astype(BF16)
        acc = acc + jnp.dot(act, wd_ref[j * fc:(j + 1) * fc, :], preferred_element_type=F32)
    o_ref[0] = x1 + _rms(acc, NORM_EPS) * gpost_ref[...]


def _post(yc, x0, ya, x, wo_bf, gmix, gpre, wu_bf, wd_bf, gpost, tm, fc):
    b, s, d = x.shape
    c = x0.shape[2]
    row = lambda n: pl.BlockSpec((1, tm, n), lambda bi, i: (bi, i, 0))
    vec = pl.BlockSpec((1, d), lambda bi, i: (0, 0))
    resident = lambda a: pl.BlockSpec(a.shape, lambda bi, i: (0, 0), pipeline_mode=pl.Buffered(1))
    return pl.pallas_call(
        functools.partial(_post_kernel, c=c, n2=_fft_split(s)[1], fc=fc),
        grid=(b, s // tm),
        in_specs=[pl.BlockSpec((1, c // LANES, _padded_rows(tm, s), LANES), lambda bi, i: (bi, 0, i, 0)),
                  row(c), pl.BlockSpec((1, ya.shape[1], tm, HEAD_WIDTH), lambda bi, i: (bi, 0, i, 0)), row(d),
                  resident(wo_bf), vec, vec, resident(wu_bf), resident(wd_bf), vec],
        out_specs=row(d),
        out_shape=jax.ShapeDtypeStruct((b, s, d), F32),
        compiler_params=_cparams(("parallel", "parallel"), 56),
        name="outproj_mlp",
    )(yc, x0, ya, x, wo_bf, gmix, gpre, wu_bf, wd_bf, gpost)


def _tile(n, want):
    return min(n, want)


def kernel(x, attn_pre_gain, attn_post_gain, w_in, conv_w, conv_b, filt_w1, filt_b1, filt_w2, filt_b2, filt_w3, filt_b3, filt_w4, filt_freq, filt_bias, lam_q1, lam_k1, lam_q2, lam_k2, subln_gain, w_out, mlp_pre_gain, mlp_post_gain, w_up, w_down):
    b, s, d = x.shape
    depth = w_in.shape[0]
    c = filt_bias.shape[1]
    hy3 = 3 * c
    aw = (w_in.shape[2] - hy3) // 3
    row = lambda a, l: a[l][None, :]
    for l in range(depth):
        g, x0, q, k, v = _inproj(x, row(attn_pre_gain, l), w_in[l].astype(BF16), conv_w[l], row(conv_b, l), c, aw,
                                 _tile(s, 1024))
        hfb = _filters(s, c, filt_w1[l], row(filt_b1, l), filt_w2[l], row(filt_b2, l), filt_w3[l],
                       row(filt_b3, l), filt_w4[l], row(filt_freq, l), _tile(s, 512))
        yc = _long_conv(g, hfb, row(filt_bias, l), s)
        lam_init = 0.8 - 0.6 * math.exp(-0.3 * l)
        y_at = _attention(q, k, v, row(lam_q1, l), row(lam_k1, l), row(lam_q2, l), row(lam_k2, l),
                          row(subln_gain, l), lam_init, _tile(s, 256), _tile(s, 512))
        x = _post(yc, x0, y_at, x, w_out[l].astype(BF16), row(attn_post_gain, l),
                  row(mlp_pre_gain, l), w_up[l].astype(BF16), w_down[l].astype(BF16), row(mlp_post_gain, l),
                  _tile(s, 512), _tile(w_up.shape[2], 1024))
    return x
```

```python
import functools
import math

import jax
import jax.numpy as jnp
from jax import lax
from jax.experimental import pallas as pl
from jax.experimental.pallas import tpu as pltpu

F32 = jnp.float32
BF16 = jnp.bfloat16

HEAD_DIM = 64
HEAD_WIDTH = 2 * HEAD_DIM
FILTER_EMB_DIM = 33
FILTER_TARGET = 1e-2
FILTER_FAST_DECAY_PCT = 0.3
FILTER_SLOW_DECAY_PCT = 1.5
ROPE_THETA = 10000.0
NORM_EPS = 1e-6
SUBLN_EPS = 1e-5

LANES = 128
MIB = 1024 * 1024


def _cparams(semantics, vmem_mib):
    return pltpu.CompilerParams(dimension_semantics=semantics, vmem_limit_bytes=vmem_mib * MIB)


def _rms(x, eps):
    return x * lax.rsqrt(jnp.mean(x * x, axis=-1, keepdims=True) + eps)


def _fft_split(seq):
    n1 = min(128, seq // 8)
    return n1, 2 * seq // n1


ROW_PAD = 8


def _padded_rows(rows, seq):
    n2 = _fft_split(seq)[1]
    return rows // n2 * (n2 + ROW_PAD)


def _store_groups(ref, lead, x, n2):
    pitch = n2 + ROW_PAD
    pad = jnp.zeros((ROW_PAD, x.shape[1]), x.dtype)
    for gi in range(x.shape[0] // n2):
        ref[lead + (slice(gi * pitch, gi * pitch + n2),)] = x[gi * n2:(gi + 1) * n2]
        ref[lead + (slice(gi * pitch + n2, (gi + 1) * pitch),)] = pad


HALO = 16


def _inproj_kernel(prev_ref, x_ref, next_ref, gain_ref, w_ref, cw_ref, cb_ref, cos_ref, sa_ref, sb_ref,
                   g_ref, x0_ref, q_ref, k_ref, v_ref, *, c, aw, scale, n2):
    i = pl.program_id(1)
    last = pl.num_programs(1) - 1
    hy3 = 3 * c
    tm = x_ref.shape[1]
    gain = gain_ref[...]
    normed = lambda x: (_rms(x, NORM_EPS) * gain).astype(BF16)
    hb = normed(x_ref[0])
    he = jnp.concatenate([normed(prev_ref[0]), hb, normed(next_ref[0])], axis=0)
    ue = jnp.dot(he, w_ref[:, :hy3], preferred_element_type=F32)
    zero = jnp.zeros((8, hy3), F32)
    ue = jnp.concatenate([ue[:HALO - 8], jnp.where(i > 0, ue[HALO - 8:HALO], zero), ue[HALO:HALO + tm],
                          jnp.where(i < last, ue[HALO + tm:HALO + tm + 8], zero), ue[HALO + tm + 8:]], axis=0)
    rows = ue.shape[0]
    up = pltpu.roll(ue, 1, 0)[HALO:HALO + tm]
    dn = pltpu.roll(ue, rows - 1, 0)[HALO:HALO + tm]
    y = up * cw_ref[0:1, :] + ue[HALO:HALO + tm] * cw_ref[1:2, :] + dn * cw_ref[2:3, :] + cb_ref[...]
    x0_ref[0] = y[:, :c].astype(BF16)
    g = y[:, 2 * c:] * y[:, c:2 * c]
    for t in range(c // LANES):
        _store_groups(g_ref, (0, t), g[:, t * LANES:(t + 1) * LANES], n2)
    cos, sa, sb = cos_ref[...], sa_ref[...], sb_ref[...]

    def rope(t):
        return t * cos + pltpu.roll(t, LANES - HEAD_DIM // 2, 1) * sa + pltpu.roll(t, HEAD_DIM // 2, 1) * sb

    q = jnp.dot(hb, w_ref[:, hy3:hy3 + aw], preferred_element_type=F32)
    k = jnp.dot(hb, w_ref[:, hy3 + aw:hy3 + 2 * aw], preferred_element_type=F32)
    v = jnp.dot(hb, w_ref[:, hy3 + 2 * aw:], preferred_element_type=F32)
    for h in range(aw // HEAD_WIDTH):
        sl = slice(h * HEAD_WIDTH, (h + 1) * HEAD_WIDTH)
        q_ref[0, h] = (rope(q[:, sl]) * scale).astype(BF16)
        k_ref[0, h] = rope(k[:, sl]).astype(BF16)
        v_ref[0, h] = v[:, sl].astype(BF16)


def _rope_tables(seq):
    inv = ROPE_THETA ** (-jnp.arange(0, HEAD_DIM, 2, dtype=F32) / HEAD_DIM)
    ang = jnp.arange(seq, dtype=F32)[:, None] * inv[None, :]
    ang = jnp.tile(ang, (1, LANES // (HEAD_DIM // 2)))
    first = (jnp.arange(LANES) % HEAD_DIM) < HEAD_DIM // 2
    sin = jnp.sin(ang)
    return jnp.cos(ang), jnp.where(first, -sin, 0.0), jnp.where(first, 0.0, sin)


def _inproj(x, gain, w_bf, conv_w, conv_b, c, aw, tm):
    b, seq, d = x.shape
    heads = aw // HEAD_WIDTH
    nct = c // LANES
    cos, sa, sb = _rope_tables(seq)
    nb, nhalo = tm // HALO, seq // HALO
    tab = pl.BlockSpec((tm, LANES), lambda bi, i: (i, 0))
    row = lambda n: pl.BlockSpec((1, tm, n), lambda bi, i: (bi, i, 0))
    full = lambda a: pl.BlockSpec(a.shape, lambda bi, i: (0, 0))
    head_major = pl.BlockSpec((1, heads, tm, HEAD_WIDTH), lambda bi, i: (bi, 0, i, 0))
    return pl.pallas_call(
        functools.partial(_inproj_kernel, c=c, aw=aw, scale=HEAD_DIM ** -0.5 * math.log2(math.e),
                          n2=_fft_split(seq)[1]),
        grid=(b, seq // tm),
        in_specs=[pl.BlockSpec((1, HALO, d), lambda bi, i: (bi, jnp.maximum(i * nb - 1, 0), 0)), row(d),
                  pl.BlockSpec((1, HALO, d), lambda bi, i: (bi, jnp.minimum((i + 1) * nb, nhalo - 1), 0)),
                  full(gain), full(w_bf), full(conv_w), full(conv_b), tab, tab, tab],
        out_specs=[pl.BlockSpec((1, nct, _padded_rows(tm, seq), LANES), lambda bi, i: (bi, 0, i, 0)), row(c),
                   head_major, head_major, head_major],
        out_shape=[jax.ShapeDtypeStruct((b, nct, _padded_rows(seq, seq), LANES), F32),
                   jax.ShapeDtypeStruct((b, seq, c), BF16)]
        + [jax.ShapeDtypeStruct((b, heads, seq, HEAD_WIDTH), BF16)] * 3,
        compiler_params=_cparams(("parallel", "parallel"), 56),
        name="inproj",
    )(x, x, x, gain, w_bf, conv_w, conv_b, cos, sa, sb)


def _filter_kernel(z_ref, dec_ref, w1_ref, b1_ref, w2_ref, b2_ref, w3_ref, b3_ref, w4_ref, fr_ref, o_ref, *, c, n2):
    dot = functools.partial(jnp.dot, preferred_element_type=F32, precision=lax.Precision.HIGHEST)
    fr = fr_ref[...]
    h = jnp.sin(fr * (dot(z_ref[...], w1_ref[...]) + b1_ref[...]))
    h = jnp.sin(fr * (dot(h, w2_ref[...]) + b2_ref[...]))
    h = jnp.sin(fr * (dot(h, w3_ref[...]) + b3_ref[...]))
    h = jnp.dot(h.astype(BF16), w4_ref[...].astype(BF16), preferred_element_type=F32)
    dec = dec_ref[...]
    for d in range(2):
        hd = h[:, d * c:(d + 1) * c] * dec
        for t in range(c // LANES):
            _store_groups(o_ref, (d, t), hd[:, t * LANES:(t + 1) * LANES], n2)


def _filter_inputs(seq, c):
    bands = (FILTER_EMB_DIM - 1) // 2
    t = jnp.linspace(0.0, 1.0, seq, dtype=F32)[:, None]
    w = (2.0 * math.pi / seq) * jnp.arange(seq, dtype=F32)[:, None]
    f = jnp.linspace(1e-4, bands - 1, bands, dtype=F32)[None, :]
    fw = f * w
    z = jnp.concatenate([t, jnp.cos(fw), -jnp.sin(fw)], axis=-1)
    max_decay = math.log(FILTER_TARGET) / FILTER_FAST_DECAY_PCT
    min_decay = math.log(FILTER_TARGET) / FILTER_SLOW_DECAY_PCT
    deltas = jnp.linspace(min_decay, max_decay, c, dtype=F32)[None, :]
    return z, jnp.exp(-t * jnp.abs(deltas))


def _filters(seq, c, w1, b1, w2, b2, w3, b3, w4, freq, ts):
    z, decay = _filter_inputs(seq, c)
    order = w1.shape[1]
    zp = jnp.pad(z, ((0, 0), (0, order - FILTER_EMB_DIM)))
    w1p = jnp.pad(w1, ((0, order - FILTER_EMB_DIM), (0, 0)))
    full = lambda a: pl.BlockSpec(a.shape, lambda i: (0,) * a.ndim)
    args = (zp, decay, w1p, b1, w2, b2, w3, b3, w4, freq)
    return pl.pallas_call(
        functools.partial(_filter_kernel, c=c, n2=_fft_split(seq)[1]),
        grid=(seq // ts,),
        in_specs=[pl.BlockSpec((ts, order), lambda i: (i, 0)), pl.BlockSpec((ts, c), lambda i: (i, 0))]
        + [full(a) for a in args[2:]],
        out_specs=pl.BlockSpec((2, c // LANES, _padded_rows(ts, seq), LANES), lambda i: (0, 0, i, 0)),
        out_shape=jax.ShapeDtypeStruct((2, c // LANES, _padded_rows(seq, seq), LANES), F32),
        compiler_params=_cparams(("parallel",), 48),
        name="hyfilter",
    )(*args)


def _cis(m, n):
    ang = (m % n).astype(F32) * (2.0 * math.pi / n)
    return jnp.cos(ang), jnp.sin(ang)


def _dft_tables(n1, n2):
    n = n1 * n2
    n1h = n1 // 2
    k1 = jnp.arange(n1, dtype=jnp.int32)[:, None]
    j1 = jnp.arange(n1h, dtype=jnp.int32)[None, :]
    c, s = _cis(k1 * j1, n1)
    fa = jnp.stack([c, -s], axis=1).reshape(2 * n1, n1h)
    fb = jnp.stack([s, c], axis=1).reshape(2 * n1, n1h)
    f1 = jnp.concatenate([fa, fb], axis=1)
    j2 = jnp.arange(n2, dtype=jnp.int32)[None, :]
    ca, sa = _cis(jnp.arange(n1, dtype=jnp.int32)[:, None] * j2, n)
    cb, sb = _cis(jnp.arange(n2, dtype=jnp.int32)[:, None] * j2, n2)
    ca, sa, cb, sb = ca[:, None, :], sa[:, None, :], cb[None], sb[None]
    c, s = ca * cb - sa * sb, sa * cb + ca * sb
    m2 = jnp.concatenate([jnp.concatenate([c, s], axis=2), jnp.concatenate([-s, c], axis=2)], axis=1)
    ct, st = jnp.swapaxes(c, 1, 2), jnp.swapaxes(s, 1, 2)
    m2i = jnp.concatenate([jnp.concatenate([ct, -st], axis=2), jnp.concatenate([st, ct], axis=2)], axis=1)
    j1 = jnp.arange(n1h, dtype=jnp.int32)[:, None]
    k1 = jnp.arange(n1, dtype=jnp.int32)[None, :]
    c, s = _cis(j1 * k1, n1)
    c, s = c / n, s / n
    g_re = jnp.stack([c, -s], axis=2).reshape(n1h, 2 * n1)
    g_im = jnp.stack([s, c], axis=2).reshape(n1h, 2 * n1)
    g1 = jnp.concatenate([g_re, g_im], axis=0)
    return f1.astype(BF16), m2.astype(BF16), m2i.astype(BF16), g1.astype(BF16)


FFT_UNROLL = 64


def _re_im_rows(k1, n2):
    pitch = n2 + ROW_PAD
    return (pl.ds(pl.multiple_of(k1 * 2 * pitch, 8), n2), pl.ds(pl.multiple_of(k1 * 2 * pitch + pitch, 8), n2))


def _hyspec_kernel(fa_ref, m2_ref, bias_ref, hf_ref, hb_ref, h_ref, af_ref, ab_ref, *, n1, n2):
    n1h = n1 // 2
    pitch = n2 + ROW_PAD
    fa = fa_ref[...]

    def stage1(j2, carry):
        src = pl.ds(j2, n1h, stride=pitch)
        dst = pl.ds(j2, 2 * n1, stride=pitch)
        af_ref[dst, :] = jnp.dot(fa, hf_ref[0, 0, src, :].astype(BF16), preferred_element_type=F32)
        ab_ref[dst, :] = jnp.dot(fa, hb_ref[0, 0, src, :].astype(BF16), preferred_element_type=F32)
        return carry

    lax.fori_loop(0, n2, stage1, 0, unroll=min(FFT_UNROLL, n2))
    bias = bias_ref[...]

    def stage2(k1, carry):
        re, im = _re_im_rows(k1, n2)
        sf = jnp.concatenate([af_ref[re, :], af_ref[im, :]], axis=0).astype(BF16)
        sb = jnp.concatenate([ab_ref[re, :], ab_ref[im, :]], axis=0).astype(BF16)
        xf = jnp.dot(m2_ref[k1], sf, preferred_element_type=F32)
        xb = jnp.dot(m2_ref[k1], sb, preferred_element_type=F32)
        h_ref[k1, :n2, :] = (xf[:n2] + xb[:n2] + bias).astype(BF16)
        h_ref[k1, n2:, :] = (xf[n2:] - xb[n2:]).astype(BF16)
        return carry

    lax.fori_loop(0, n1, stage2, 0, unroll=min(FFT_UNROLL, n1))


def _hyconv_kernel(f1_ref, m2_ref, m2i_ref, g1_ref, h_ref, ga_ref, gb_ref, o_ref, a_ref, *, n1, n2):
    n1h = n1 // 2
    pitch = n2 + ROW_PAD
    f1 = f1_ref[...]
    pad = jnp.zeros((ROW_PAD, LANES), F32)
    for r in range(2):
        for gi in range(n1h):
            o_ref[r, 0, 0, gi * pitch + n2:(gi + 1) * pitch, :] = pad

    def stage1(j2, carry):
        src = pl.ds(j2, n1h, stride=pitch)
        z = jnp.concatenate([ga_ref[0, 0, src, :], gb_ref[0, 0, src, :]], axis=0).astype(BF16)
        a_ref[pl.ds(j2, 2 * n1, stride=pitch), :] = jnp.dot(f1, z, preferred_element_type=F32)
        return carry

    lax.fori_loop(0, n2, stage1, 0, unroll=min(FFT_UNROLL, n2))

    def stage2(k1, carry):
        re, im = _re_im_rows(k1, n2)
        a = jnp.concatenate([a_ref[re, :], a_ref[im, :]], axis=0).astype(BF16)
        x = jnp.dot(m2_ref[k1], a, preferred_element_type=F32)
        xr, xi = x[:n2], x[n2:]
        hr, hi = h_ref[k1, :n2, :].astype(F32), h_ref[k1, n2:, :].astype(F32)
        y = jnp.concatenate([xr * hr - xi * hi, xr * hi + xi * hr], axis=0).astype(BF16)
        b = jnp.dot(m2i_ref[k1], y, preferred_element_type=F32)
        a_ref[re, :] = b[:n2]
        a_ref[im, :] = b[n2:]
        return carry

    lax.fori_loop(0, n1, stage2, 0, unroll=min(FFT_UNROLL, n1))
    g1 = g1_ref[...]

    def stage3(j2, carry):
        bs = a_ref[pl.ds(j2, 2 * n1, stride=pitch), :].astype(BF16)
        y = jnp.dot(g1, bs, preferred_element_type=F32)
        dst = pl.ds(j2, n1h, stride=pitch)
        o_ref[0, 0, 0, dst, :] = y[:n1h]
        o_ref[1, 0, 0, dst, :] = y[n1h:]
        return carry

    lax.fori_loop(0, n2, stage3, 0, unroll=min(FFT_UNROLL, n2))


def _long_conv(g, hfb, filt_bias, seq):
    b, nct, seqp, _ = g.shape
    pairs = b // 2
    n1, n2 = _fft_split(seq)
    f1, m2, m2i, g1 = _dft_tables(n1, n2)
    fa = f1[:, :n1 // 2]
    rows = 2 * n1 * (n2 + ROW_PAD)
    const = lambda a, nd: pl.BlockSpec(a.shape, (lambda t: (0,) * a.ndim) if nd == 1 else (lambda t, p: (0,) * a.ndim),
                                       pipeline_mode=pl.Buffered(1))
    h = pl.pallas_call(
        functools.partial(_hyspec_kernel, n1=n1, n2=n2),
        grid=(nct,),
        in_specs=[const(fa, 1), const(m2, 1), pl.BlockSpec((1, LANES), lambda t: (0, t)),
                  pl.BlockSpec((1, 1, seqp, LANES), lambda t: (0, t, 0, 0)),
                  pl.BlockSpec((1, 1, seqp, LANES), lambda t: (1, t, 0, 0))],
        out_specs=pl.BlockSpec((n1, 2 * n2, LANES), lambda t: (0, 0, t)),
        out_shape=jax.ShapeDtypeStruct((n1, 2 * n2, nct * LANES), BF16),
        scratch_shapes=[pltpu.VMEM((rows, LANES), F32), pltpu.VMEM((rows, LANES), F32)],
        compiler_params=_cparams(("parallel",), 48),
        name="hyspectrum",
    )(fa, m2, filt_bias, hfb, hfb)
    seqblk = lambda off: pl.BlockSpec((1, 1, seqp, LANES), lambda t, p: (p + off, t, 0, 0))
    y = pl.pallas_call(
        functools.partial(_hyconv_kernel, n1=n1, n2=n2),
        grid=(nct, pairs),
        in_specs=[const(f1, 2), const(m2, 2), const(m2i, 2), const(g1, 2),
                  pl.BlockSpec((n1, 2 * n2, LANES), lambda t, p: (0, 0, t)), seqblk(0), seqblk(pairs)],
        out_specs=pl.BlockSpec((2, 1, 1, seqp, LANES), lambda t, p: (0, p, t, 0, 0)),
        out_shape=jax.ShapeDtypeStruct((2, pairs, nct, seqp, LANES), F32),
        scratch_shapes=[pltpu.VMEM((rows, LANES), F32)],
        compiler_params=_cparams(("parallel", "parallel"), 48),
        name="hyconv",
    )(f1, m2, m2i, g1, h, g, g)
    return y.reshape(b, nct, seqp, LANES)


def _lane_fold(x, op):
    acc = x[:, :LANES]
    for t in range(1, x.shape[1] // LANES):
        acc = op(acc, x[:, t * LANES:(t + 1) * LANES])
    return acc


ANCHOR_LAG = 4


def _row_fold(x, op):
    parts = [x[t * 8:(t + 1) * 8] for t in range(x.shape[0] // 8)]
    while len(parts) > 1:
        parts = [op(parts[t], parts[t + 1]) for t in range(0, len(parts) - 1, 2)] + parts[len(parts) & ~1:]
    return parts[0]


def _attn_kernel(par_ref, q_ref, k_ref, v_ref, o_ref, s_ref, p_ref, m_ref, l_ref, vt_ref, *, lam_init, qb, kc):
    heads, seq = k_ref.shape[1], k_ref.shape[2]
    nk, nb = seq // kc, seq // qb
    lam = (jnp.exp(jnp.sum(par_ref[0:1, :] * par_ref[1:2, :], axis=-1, keepdims=True))
           - jnp.exp(jnp.sum(par_ref[2:3, :] * par_ref[3:4, :], axis=-1, keepdims=True)) + lam_init)
    subln_gain = par_ref[4:5, :]
    lane = lax.broadcasted_iota(jnp.int32, (qb, HEAD_WIDTH), 1)
    zero = jnp.zeros((qb, HEAD_WIDTH), BF16)
    for h in range(heads):
        for j in range(nk):
            keys = slice(j * kc, (j + 1) * kc)
            vt_ref[h, :, keys] = v_ref[0, h, keys, :].astype(F32).T.astype(BF16)

    def head_rows(f):
        h, r = f // nb, f % nb
        return h, pl.ds(pl.multiple_of(r * qb, qb), qb)

    def stage(i_values, i_exps, i_scores):
        if i_values is not None:
            hv, rows_v = head_rows(i_values)
            l1 = jnp.sum(l_ref[0], axis=0, keepdims=True)
            l2 = jnp.sum(l_ref[1], axis=0, keepdims=True)
            ratio = (lam * l1 / l2).astype(BF16)
            ot = jnp.zeros((HEAD_WIDTH, qb), F32)
        if i_exps is not None:
            colmax = [jnp.max(m_ref[c], axis=0, keepdims=True) for c in range(2)]
            lacc = [jnp.zeros((8, qb), F32) for _ in range(2)]
            anchors = []
        if i_scores is not None:
            hs, rows_s = head_rows(i_scores)
            q = q_ref[0, hs, rows_s, :]
            qc = (jnp.where(lane < HEAD_DIM, q, zero), jnp.where(lane < HEAD_DIM, zero, q))
            macc = [jnp.full((8, qb), -jnp.inf, F32) for _ in range(2)]
        for j in range(nk):
            keys = slice(j * kc, (j + 1) * kc)
            tie = None
            if i_exps is not None and j > 0:
                both = jnp.max(jnp.maximum(lacc[0], lacc[1]), axis=0, keepdims=True)
                anchors.append(jnp.where(_lane_fold(both, jnp.maximum) > 1e30, 1.0, 0.0).astype(BF16))
                if len(anchors) >= ANCHOR_LAG:
                    tie = anchors[-ANCHOR_LAG]
            if i_values is not None:
                wt = p_ref[0, j] - p_ref[1, j] * ratio
                ot = ot + jnp.dot(vt_ref[hv, :, keys], wt, preferred_element_type=F32)
            if i_exps is not None:
                for c in range(2):
                    e = jnp.exp2(s_ref[c, j] - colmax[c])
                    p_ref[c, j] = e.astype(BF16)
                    lacc[c] = lacc[c] + _row_fold(e, jnp.add)
            if i_scores is not None:
                for c in range(2):
                    qcj = qc[c] if tie is None else qc[c] + tie
                    s = lax.dot_general(k_ref[0, hs, keys, :], qcj, (((1,), (1,)), ((), ())),
                                        preferred_element_type=F32)
                    s_ref[c, j] = s
                    macc[c] = jnp.maximum(macc[c], _row_fold(s, jnp.maximum))
        for c in range(2):
            if i_exps is not None:
                l_ref[c] = lacc[c]
            if i_scores is not None:
                m_ref[c] = macc[c]
        if i_values is not None:
            o = (ot * (1.0 / l1)).T
            o = _rms(o, SUBLN_EPS) * subln_gain * (1.0 - lam_init)
            o_ref[0, hv, rows_v, :] = o.astype(BF16)

    def steady(i, carry):
        stage(i - 1, i, i + 1)
        return carry

    nf = heads * nb
    stage(None, None, 0)
    stage(None, 0, 1)
    lax.fori_loop(1, nf - 1, steady, 0)
    stage(nf - 2, nf - 1, None)
    stage(nf - 1, None, None)


def _attention(q, k, v, lq1, lk1, lq2, lk2, subln_gain, lam_init, qb, kc):
    b, heads, s, _ = q.shape
    assert heads * (s // qb) >= 2
    pad = lambda a: jnp.pad(a, ((0, 0), (0, LANES - a.shape[1])))
    par = jnp.concatenate([pad(lq1), pad(lk1), pad(lq2), pad(lk2), subln_gain, jnp.zeros((11, LANES), F32)], axis=0)
    seqblk = pl.BlockSpec((1, heads, s, HEAD_WIDTH), lambda bi: (bi, 0, 0, 0))
    return pl.pallas_call(
        functools.partial(_attn_kernel, lam_init=lam_init, qb=qb, kc=kc),
        grid=(b,),
        in_specs=[pl.BlockSpec(par.shape, lambda bi: (0, 0)), seqblk, seqblk, seqblk],
        out_specs=seqblk,
        out_shape=jax.ShapeDtypeStruct((b, heads, s, HEAD_WIDTH), BF16),
        scratch_shapes=[pltpu.VMEM((2, s // kc, kc, qb), F32), pltpu.VMEM((2, s // kc, kc, qb), BF16),
                        pltpu.VMEM((2, 8, qb), F32), pltpu.VMEM((2, 8, qb), F32),
                        pltpu.VMEM((heads, HEAD_WIDTH, s), BF16)],
        compiler_params=_cparams(("parallel",), 58),
        name="diffattn",
    )(par, q, k, v)


def _post_kernel(yc_ref, x0_ref, ya_ref, x_ref, wo_ref, gmix_ref, gpre_ref, wu_ref, wd_ref, gpost_ref, o_ref,
                 *, c, n2, fc):
    pitch = n2 + ROW_PAD
    groups = range(x0_ref.shape[1] // n2)
    conv = jnp.concatenate([jnp.concatenate([yc_ref[0, t, gi * pitch:gi * pitch + n2, :] for gi in groups], axis=0)
                            for t in range(c // LANES)], axis=1)
    yh = (conv * x0_ref[0].astype(F32)).astype(BF16)
    ya = jnp.concatenate([ya_ref[0, h] for h in range(ya_ref.shape[1])], axis=1)
    mix = (jnp.dot(yh, wo_ref[:c, :], preferred_element_type=F32)
           + jnp.dot(ya, wo_ref[c:, :], preferred_element_type=F32))
    x1 = x_ref[0] + _rms(mix, NORM_EPS) * gmix_ref[...]
    hb = (_rms(x1, NORM_EPS) * gpre_ref[...]).astype(BF16)
    acc = jnp.zeros(x1.shape, F32)
    for j in range(wu_ref.shape[1] // fc):
        up = jnp.dot(hb, wu_ref[:, j * fc:(j + 1) * fc], preferred_element_type=F32)
        act = jnp.square(jnp.maximum(up, 0.0)).astype(BF16)
        acc = acc + jnp.dot(act, wd_ref[j * fc:(j + 1) * fc, :], preferred_element_type=F32)
    o_ref[0] = x1 + _rms(acc, NORM_EPS) * gpost_ref[...]


def _post(yc, x0, ya, x, wo_bf, gmix, gpre, wu_bf, wd_bf, gpost, tm, fc):
    b, s, d = x.shape
    c = x0.shape[2]
    row = lambda n: pl.BlockSpec((1, tm, n), lambda bi, i: (bi, i, 0))
    vec = pl.BlockSpec((1, d), lambda bi, i: (0, 0))
    resident = lambda a: pl.BlockSpec(a.shape, lambda bi, i: (0, 0), pipeline_mode=pl.Buffered(1))
    return pl.pallas_call(
        functools.partial(_post_kernel, c=c, n2=_fft_split(s)[1], fc=fc),
        grid=(b, s // tm),
        in_specs=[pl.BlockSpec((1, c // LANES, _padded_rows(tm, s), LANES), lambda bi, i: (bi, 0, i, 0)),
                  row(c), pl.BlockSpec((1, ya.shape[1], tm, HEAD_WIDTH), lambda bi, i: (bi, 0, i, 0)), row(d),
                  resident(wo_bf), vec, vec, resident(wu_bf), resident(wd_bf), vec],
        out_specs=row(d),
        out_shape=jax.ShapeDtypeStruct((b, s, d), F32),
        compiler_params=_cparams(("parallel", "parallel"), 56),
        name="outproj_mlp",
    )(yc, x0, ya, x, wo_bf, gmix, gpre, wu_bf, wd_bf, gpost)


def _tile(n, want):
    return min(n, want)


def kernel(x, attn_pre_gain, attn_post_gain, w_in, conv_w, conv_b, filt_w1, filt_b1, filt_w2, filt_b2, filt_w3, filt_b3, filt_w4, filt_freq, filt_bias, lam_q1, lam_k1, lam_q2, lam_k2, subln_gain, w_out, mlp_pre_gain, mlp_post_gain, w_up, w_down):
    b, s, d = x.shape
    depth = w_in.shape[0]
    c = filt_bias.shape[1]
    hy3 = 3 * c
    aw = (w_in.shape[2] - hy3) // 3
    row = lambda a, l: a[l][None, :]
    for l in range(depth):
        g, x0, q, k, v = _inproj(x, row(attn_pre_gain, l), w_in[l].astype(BF16), conv_w[l], row(conv_b, l), c, aw,
                                 _tile(s, 1024))
        hfb = _filters(s, c, filt_w1[l], row(filt_b1, l), filt_w2[l], row(filt_b2, l), filt_w3[l],
                       row(filt_b3, l), filt_w4[l], row(filt_freq, l), _tile(s, 512))
        yc = _long_conv(g, hfb, row(filt_bias, l), s)
        lam_init = 0.8 - 0.6 * math.exp(-0.3 * l)
        y_at = _attention(q, k, v, row(lam_q1, l), row(lam_k1, l), row(lam_q2, l), row(lam_k2, l),
                          row(subln_gain, l), lam_init, _tile(s, 256), _tile(s, 512))
        x = _post(yc, x0, y_at, x, w_out[l].astype(BF16), row(attn_post_gain, l),
                  row(mlp_pre_gain, l), w_up[l].astype(BF16), w_down[l].astype(BF16), row(mlp_post_gain, l),
                  _tile(s, 512), _tile(w_up.shape[2], 1024))
    return x
```
